```python
import jax, jax.numpy as jnp
from jax import lax
import numpy as np

D_MODEL = 1024
BATCH = 2
SEQ = 8192
DEPTH = 1

N_META = 16
CHUNK = 128
PAD = CHUNK - N_META
RET_HEADS = 4
RET_DK = D_MODEL // RET_HEADS
RET_DV = 2 * RET_DK
SB_HEADS = D_MODEL // 64
SB_DH = 64
D_FF = -(-8 * D_MODEL // (3 * 256)) * 256
ROPE_BASE = 10000.0
NORM_EPS = 1e-6
GN_EPS = 1e-5
PROJ_SPLIT = (
    RET_HEADS * RET_DK,
    RET_HEADS * RET_DK,
    RET_HEADS * RET_DV,
    RET_HEADS * RET_DV,
    SB_HEADS * SB_DH,
    SB_HEADS * SB_DH,
    SB_HEADS * SB_DH,
    D_MODEL,
    D_MODEL,
)
PROJ_WIDTH = sum(PROJ_SPLIT)

kernel_name = "hybrid_retention_stickbreaking_block"


def rmsnorm(x, g):
    xf = x.astype(jnp.float32)
    y = xf * lax.rsqrt(jnp.mean(xf * xf, axis=-1, keepdims=True) + NORM_EPS)
    return (y * g.astype(jnp.float32)).astype(x.dtype)


def rotary(x, pos):
    half = x.shape[-1] // 2
    inv = ROPE_BASE ** (-jnp.arange(half, dtype=jnp.float32) / half)
    ang = pos[:, None] * inv[None, :]
    cos, sin = jnp.cos(ang), jnp.sin(ang)
    x1, x2 = x[..., :half], x[..., half:]
    return jnp.concatenate([x1 * cos - x2 * sin, x1 * sin + x2 * cos], axis=-1)


def retention_chunkwise(q, k, v):
    B, H, Lp, dk = q.shape
    dv = v.shape[-1]
    n = Lp // CHUNK
    log_g = jnp.log1p(-(2.0 ** (-5.0 - jnp.arange(H, dtype=jnp.float32))))
    idx = jnp.arange(CHUNK, dtype=jnp.float32)
    diff = idx[:, None] - idx[None, :]
    decay = jnp.where(diff >= 0, jnp.exp(log_g[:, None, None] * jnp.maximum(diff, 0.0)), 0.0)
    zeta = jnp.exp(log_g[:, None] * (CHUNK - 1.0 - idx))[:, :, None]
    xi = jnp.exp(log_g[:, None] * (idx + 1.0))[:, :, None]
    g_chunk = jnp.exp(log_g * CHUNK)[:, None, None]

    qc = q.reshape(B, H, n, CHUNK, dk)
    kc = k.reshape(B, H, n, CHUNK, dk)
    vc = v.reshape(B, H, n, CHUNK, dv)
    scores = jnp.einsum('bhncd,bhnsd->bhncs', qc, kc) * decay[:, None]
    inner = jnp.einsum('bhncs,bhnse->bhnce', scores, vc)

    def step(state, inp):
        q_i, k_i, v_i = inp
        cross = jnp.einsum('bhcd,bhde->bhce', q_i, state) * xi
        state = g_chunk * state + jnp.einsum('bhsd,bhse->bhde', k_i, v_i * zeta)
        return state, cross

    s0 = jnp.zeros((B, H, dk, dv), jnp.float32)
    _, cross = lax.scan(step, s0, (qc.transpose(2, 0, 1, 3, 4), kc.transpose(2, 0, 1, 3, 4), vc.transpose(2, 0, 1, 3, 4)))
    out = inner + cross.transpose(1, 2, 0, 3, 4)
    return out.reshape(B, H, Lp, dv)


def head_groupnorm(y):
    mu = jnp.mean(y, axis=-1, keepdims=True)
    var = jnp.mean(jnp.square(y - mu), axis=-1, keepdims=True)
    return (y - mu) * lax.rsqrt(var + GN_EPS)


def stick_breaking(q, k, v):
    Lp, d = q.shape[2], q.shape[3]
    scale = d ** -0.5
    outs = []
    for i in range(Lp // CHUNK):
        lk = (i + 1) * CHUNK
        q_blk = q[:, :, i * CHUNK:lk]
        k_pre, v_pre = k[:, :, :lk], v[:, :, :lk]
        z = jnp.einsum('bhtd,bhsd->bhts', q_blk, k_pre) * scale
        t_pos = i * CHUNK + jnp.arange(CHUNK)
        s_pos = jnp.arange(lk)
        valid = (s_pos[None, :] < t_pos[:, None]) & (s_pos[None, :] >= PAD)
        log_not = jnp.where(valid, jax.nn.log_sigmoid(-z), 0.0)
        after = lax.cumsum(log_not, axis=3, reverse=True) - log_not
        a = jnp.where(valid, jnp.exp(jax.nn.log_sigmoid(z) + after), 0.0)
        outs.append(jnp.einsum('bhts,bhse->bhte', a, v_pre))
    return jnp.concatenate(outs, axis=2)


def hybrid_mixer(hn, w_in, w_ret_out, w_sb_out, w_out):
    B, L, _ = hn.shape
    dtype = hn.dtype
    proj = hn @ w_in
    rq, rk, rv, rg, sq, sk, sv, ga, gb = jnp.split(proj, list(np.cumsum(PROJ_SPLIT)[:-1]), axis=-1)

    def heads(t, h):
        t = t.reshape(B, L, h, -1).transpose(0, 2, 1, 3).astype(jnp.float32)
        return jnp.pad(t, ((0, 0), (0, 0), (PAD, 0), (0, 0)))

    def merge_heads(t):
        t = t[:, :, PAD:]
        return t.transpose(0, 2, 1, 3).reshape(B, L, -1)

    pos = jnp.arange(L + PAD, dtype=jnp.float32) - PAD
    q_r = rotary(heads(rq, RET_HEADS), pos) * (RET_DK ** -0.5)
    k_r = rotary(heads(rk, RET_HEADS), pos)
    y_ret = head_groupnorm(retention_chunkwise(q_r, k_r, heads(rv, RET_HEADS)))
    y_ret = (jax.nn.silu(rg.astype(jnp.float32)) * merge_heads(y_ret)).astype(dtype) @ w_ret_out
    y_sb = merge_heads(stick_breaking(heads(sq, SB_HEADS), heads(sk, SB_HEADS), heads(sv, SB_HEADS)))
    y_sb = y_sb.astype(dtype) @ w_sb_out
    merged = jax.nn.sigmoid(ga) * y_ret + jax.nn.sigmoid(gb) * y_sb
    return merged @ w_out


def swiglu(hn, w_ffn_in, w_ffn_out):
    a, b = jnp.split(hn @ w_ffn_in, 2, axis=-1)
    return (jax.nn.silu(a) * b) @ w_ffn_out


def setup_inputs(seed: int = 0) -> dict:
    key = jax.random.key(seed)
    ks = jax.random.split(key, 12)
    f32 = jnp.float32
    nrm = lambda k, shape, fan: jax.random.normal(k, shape, f32) * (fan ** -0.5)
    gain = lambda k: 1.0 + 0.02 * jax.random.normal(k, (DEPTH, D_MODEL), f32)
    return {
        "x": jax.random.normal(ks[0], (BATCH, SEQ, D_MODEL), f32),
        "meta_tokens": jax.random.normal(ks[1], (N_META, D_MODEL), f32),
        "w_in": nrm(ks[2], (DEPTH, D_MODEL, PROJ_WIDTH), D_MODEL),
        "w_ret_out": nrm(ks[3], (DEPTH, RET_HEADS * RET_DV, D_MODEL), RET_HEADS * RET_DV),
        "w_sb_out": nrm(ks[4], (DEPTH, SB_HEADS * SB_DH, D_MODEL), SB_HEADS * SB_DH),
        "w_out": nrm(ks[5], (DEPTH, D_MODEL, D_MODEL), D_MODEL),
        "w_ffn_in": nrm(ks[6], (DEPTH, D_MODEL, 2 * D_FF), D_MODEL),
        "w_ffn_out": nrm(ks[7], (DEPTH, D_FF, D_MODEL), D_FF),
        "norm_mix_pre": gain(ks[8]),
        "norm_mix_post": gain(ks[9]),
        "norm_ffn_pre": gain(ks[10]),
        "norm_ffn_post": gain(ks[11]),
    }


def reference(x, meta_tokens, w_in, w_ret_out, w_sb_out, w_out, w_ffn_in, w_ffn_out,
              norm_mix_pre, norm_mix_post, norm_ffn_pre, norm_ffn_post):
    B = x.shape[0]
    meta = jnp.broadcast_to(meta_tokens.astype(x.dtype)[None], (B, N_META, x.shape[-1]))
    h = jnp.concatenate([meta, x], axis=1)
    for l in range(DEPTH):
        mix = hybrid_mixer(rmsnorm(h, norm_mix_pre[l]), w_in[l], w_ret_out[l], w_sb_out[l], w_out[l])
        h = h + rmsnorm(mix, norm_mix_post[l])
        ff = swiglu(rmsnorm(h, norm_ffn_pre[l]), w_ffn_in[l], w_ffn_out[l])
        h = h + rmsnorm(ff, norm_ffn_post[l])
    return h[:, N_META:]
```

```python
import functools

import jax
import jax.numpy as jnp
import numpy as np
from jax import lax
from jax.experimental import pallas as pl
from jax.experimental.pallas import tpu as pltpu

D_MODEL = 1024
N_META = 16
CHUNK = 128
PAD = CHUNK - N_META
RET_HEADS = 4
RET_DK = 256
RET_DV = 512
SB_HEADS = 16
SB_DH = 64
D_FF = 2816
ROPE_BASE = 10000.0
NORM_EPS = 1e-6
GN_EPS = 1e-5

OFF_RQ = 0
OFF_RK = 1024
OFF_RV = 2048
OFF_RG = 4096
OFF_SQ = 6144
OFF_SK = 7168
OFF_SV = 8192
OFF_GA = 9216
OFF_GB = 10240
PROJ_WIDTH = 11264

F32 = jnp.float32
BF16 = jnp.bfloat16

SB_STOP = 88.0
SB_WIN = 2 * CHUNK

VMEM_LIMIT = 48 * 1024 * 1024


def _rms(xf, g):
    ms = jnp.mean(xf * xf, axis=-1, keepdims=True)
    return xf * lax.rsqrt(ms + NORM_EPS) * g


def _in_proj_kernel(x_ref, g_ref, w_ref, o_ref, hn_ref):
    @pl.when(pl.program_id(1) == 0)
    def _():
        hn_ref[...] = _rms(x_ref[...], g_ref[...]).astype(BF16)

    o_ref[...] = jnp.dot(hn_ref[...], w_ref[...], preferred_element_type=F32).astype(BF16)


def _in_proj(x2d, gain, w_bf16, tm, tn=1024):
    rows = x2d.shape[0]
    return pl.pallas_call(
        _in_proj_kernel,
        grid=(rows // tm, PROJ_WIDTH // tn),
        in_specs=[
            pl.BlockSpec((tm, D_MODEL), lambda i, j: (i, 0)),
            pl.BlockSpec((1, D_MODEL), lambda i, j: (0, 0)),
            pl.BlockSpec((D_MODEL, tn), lambda i, j: (0, j)),
        ],
        out_specs=pl.BlockSpec((tm, tn), lambda i, j: (i, j)),
        out_shape=jax.ShapeDtypeStruct((rows, PROJ_WIDTH), BF16),
        scratch_shapes=[pltpu.VMEM((tm, D_MODEL), BF16)],
        compiler_params=pltpu.CompilerParams(
            dimension_semantics=("parallel", "arbitrary"),
            vmem_limit_bytes=VMEM_LIMIT),
        name="in_proj",
    )(x2d, gain, w_bf16)


def _rotate(x, cos, sin):
    half = RET_DK // 2
    x1, x2 = x[:, :half], x[:, half:]
    return jnp.concatenate([x1 * cos - x2 * sin, x1 * sin + x2 * cos], axis=1)


def _kv_state(k_rot, v, zeta):
    vz = (v * zeta).astype(BF16)
    return lax.dot_general(k_rot.astype(BF16), vz, (((0,), (0,)), ((), ())),
                           preferred_element_type=F32)


def _retention_kernel(q_ref, k_ref, v_ref, g_ref, cos_ref, sin_ref,
                      mk_ref, mv_ref, mcos_ref, msin_ref,
                      decay_ref, xi_ref, zeta_ref, gchunk_ref,
                      o_ref, state_ref):
    c = pl.program_id(1)

    @pl.when(c == 0)
    def _():
        mcos, msin = mcos_ref[...], msin_ref[...]
        for h in range(RET_HEADS):
            k = mk_ref[:, h * RET_DK:(h + 1) * RET_DK].astype(F32)
            v = mv_ref[:, h * RET_DV:(h + 1) * RET_DV].astype(F32)
            state_ref[h] = _kv_state(_rotate(k, mcos, msin), v, zeta_ref[h])

    cos, sin = cos_ref[...], sin_ref[...]
    for h in range(RET_HEADS):
        q = q_ref[0, :, h * RET_DK:(h + 1) * RET_DK].astype(F32)
        k = k_ref[0, :, h * RET_DK:(h + 1) * RET_DK].astype(F32)
        v = v_ref[0, :, h * RET_DV:(h + 1) * RET_DV].astype(F32)
        q_rot = (_rotate(q, cos, sin) * (RET_DK ** -0.5)).astype(BF16)
        k_rot = _rotate(k, cos, sin)
        scores = lax.dot_general(q_rot, k_rot.astype(BF16), (((1,), (1,)), ((), ())),
                                 preferred_element_type=F32) * decay_ref[h]
        inner = jnp.dot(scores.astype(BF16), v.astype(BF16), preferred_element_type=F32)
        state = state_ref[h]
        cross = jnp.dot(q_rot, state.astype(BF16), preferred_element_type=F32) * xi_ref[h]
        state_ref[h] = gchunk_ref[h, 0:1, :] * state + _kv_state(k_rot, v, zeta_ref[h])
        y = inner + cross
        mu = jnp.mean(y, axis=-1, keepdims=True)
        yc = y - mu
        var = jnp.mean(yc * yc, axis=-1, keepdims=True)
        yn = yc * lax.rsqrt(var + GN_EPS)
        gate = g_ref[0, :, h * RET_DV:(h + 1) * RET_DV].astype(F32)
        o_ref[0, :, h * RET_DV:(h + 1) * RET_DV] = (gate * jax.nn.sigmoid(gate) * yn).astype(BF16)


def _retention_tables():
    hh = np.arange(RET_HEADS, dtype=np.float64)
    log_g = np.log1p(-(2.0 ** (-5.0 - hh)))
    idx = np.arange(CHUNK, dtype=np.float64)
    diff = idx[:, None] - idx[None, :]
    decay = np.where(diff >= 0, np.exp(log_g[:, None, None] * np.maximum(diff, 0.0)), 0.0)
    zeta = np.exp(log_g[:, None] * (CHUNK - 1.0 - idx))[:, :, None]
    xi = np.exp(log_g[:, None] * (idx + 1.0))[:, :, None]
    gchunk = np.exp(log_g * CHUNK)[:, None, None]
    bc = lambda a, shape: jnp.asarray(np.broadcast_to(a, shape), F32)
    return (bc(decay, (RET_HEADS, CHUNK, CHUNK)),
            bc(xi, (RET_HEADS, CHUNK, RET_DV)),
            bc(zeta, (RET_HEADS, CHUNK, RET_DV)),
            bc(gchunk, (RET_HEADS, 8, RET_DV)))


def _rope_tables(seq):
    half = RET_DK // 2
    inv = ROPE_BASE ** (-jnp.arange(half, dtype=F32) / half)
    pos = jnp.arange(seq + CHUNK, dtype=F32) - PAD
    ang = pos[:, None] * inv[None, :]
    cos, sin = jnp.cos(ang), jnp.sin(ang)
    return cos[CHUNK:], sin[CHUNK:], cos[:CHUNK], sin[:CHUNK]


def _retention(proj, mproj, seq):
    batch = proj.shape[0]
    n = seq // CHUNK
    cos, sin, mcos, msin = _rope_tables(seq)
    decay, xi, zeta, gchunk = _retention_tables()
    const3 = lambda b, c: (0, 0, 0)
    const2 = lambda b, c: (0, 0)
    return pl.pallas_call(
        _retention_kernel,
        grid=(batch, n),
        in_specs=[
            pl.BlockSpec((1, CHUNK, 1024), lambda b, c: (b, c, OFF_RQ // 1024)),
            pl.BlockSpec((1, CHUNK, 1024), lambda b, c: (b, c, OFF_RK // 1024)),
            pl.BlockSpec((1, CHUNK, 2048), lambda b, c: (b, c, OFF_RV // 2048)),
            pl.BlockSpec((1, CHUNK, 2048), lambda b, c: (b, c, OFF_RG // 2048)),
            pl.BlockSpec((CHUNK, RET_DK // 2), lambda b, c: (c, 0)),
            pl.BlockSpec((CHUNK, RET_DK // 2), lambda b, c: (c, 0)),
            pl.BlockSpec((CHUNK, 1024), lambda b, c: (0, OFF_RK // 1024)),
            pl.BlockSpec((CHUNK, 2048), lambda b, c: (0, OFF_RV // 2048)),
            pl.BlockSpec((CHUNK, RET_DK // 2), const2),
            pl.BlockSpec((CHUNK, RET_DK // 2), const2),
            pl.BlockSpec((RET_HEADS, CHUNK, CHUNK), const3),
            pl.BlockSpec((RET_HEADS, CHUNK, RET_DV), const3),
            pl.BlockSpec((RET_HEADS, CHUNK, RET_DV), const3),
            pl.BlockSpec((RET_HEADS, 8, RET_DV), const3),
        ],
        out_specs=pl.BlockSpec((1, CHUNK, RET_HEADS * RET_DV), lambda b, c: (b, c, 0)),
        out_shape=jax.ShapeDtypeStruct((batch, seq, RET_HEADS * RET_DV), BF16),
        scratch_shapes=[pltpu.VMEM((RET_HEADS, RET_DK, RET_DV), F32)],
        compiler_params=pltpu.CompilerParams(
            dimension_semantics=("parallel", "arbitrary"),
            vmem_limit_bytes=VMEM_LIMIT),
        name="retention",
    )(proj, proj, proj, proj, cos, sin, mproj, mproj, mcos, msin, decay, xi, zeta, gchunk)


def _sb_kernel(q_ref, k_ref, v_ref, mk_ref, mv_ref, tri_ref, o_ref):
    nq = q_ref.shape[1] // CHUNK
    lane = lax.broadcasted_iota(jnp.int32, (CHUNK, 2 * SB_DH), 1)
    head0 = lane < SB_DH
    t_loc = lax.broadcasted_iota(jnp.int32, (CHUNK, SB_WIN), 0)
    s_loc = lax.broadcasted_iota(jnp.int32, (CHUNK, SB_WIN), 1)
    tri = tri_ref[...]

    def load_block(ref, mref, j):
        start = pl.multiple_of(jnp.maximum(j, 0) * CHUNK, CHUNK)
        return jnp.where(j < 0, mref[...], ref[0, pl.ds(start, CHUNK), :])

    def q_block(i, carry):
        row0 = pl.multiple_of(i * CHUNK, CHUNK)
        qf = q_ref[0, pl.ds(row0, CHUNK), :].astype(F32) * (SB_DH ** -0.5)
        qh = (jnp.where(head0, qf, 0.0).astype(BF16), jnp.where(head0, 0.0, qf).astype(BF16))
        t_pos = i * CHUNK + t_loc

        def window(state):
            j_hi, r0, r1, acc = state
            j_lo = j_hi - 1
            kw = jnp.concatenate([load_block(k_ref, mk_ref, j_lo), load_block(k_ref, mk_ref, j_hi)], axis=0)
            vw = jnp.concatenate([load_block(v_ref, mv_ref, j_lo), load_block(v_ref, mv_ref, j_hi)], axis=0)
            s_pos = j_lo * CHUNK + s_loc
            valid = (s_pos < t_pos) & (s_pos >= -N_META)
            outs, rs = [], []
            for q_h, r in zip(qh, (r0, r1)):
                z = lax.dot_general(q_h, kw, (((1,), (1,)), ((), ())), preferred_element_type=F32)
                sp = jnp.maximum(z, 0.0) + jnp.log(1.0 + jnp.exp(-jnp.abs(z)))
                ln = jnp.where(valid, sp, 0.0)
                ln_hi = ln.astype(BF16)
                ln_lo = (ln - ln_hi.astype(F32)).astype(BF16)
                cs = (jnp.dot(ln_hi, tri, preferred_element_type=F32)
                      + jnp.dot(ln_lo, tri, preferred_element_type=F32))
                after = cs[:, :SB_WIN] + jnp.concatenate([r, r], axis=1)
                a = jnp.where(valid, jnp.exp(z - sp - after), 0.0)
                outs.append(jnp.dot(a.astype(BF16), vw, preferred_element_type=F32))
                rs.append(r + cs[:, SB_WIN:])
            acc = acc + jnp.where(head0, outs[0], outs[1])
            return j_hi - 2, rs[0], rs[1], acc

        def more(state):
            j_hi, r0, r1, _ = state
            return (j_hi >= -1) & (jnp.min(jnp.minimum(r0, r1)) < SB_STOP)

        zeros = jnp.zeros((CHUNK, 2 * SB_DH), F32)
        _, _, _, acc = lax.while_loop(more, window, (i, zeros, zeros, zeros))
        o_ref[0, pl.ds(row0, CHUNK), :] = acc.astype(BF16)
        return carry

    lax.fori_loop(0, nq, q_block, 0)


def _sb_tables():
    j = np.arange(SB_WIN)[:, None]
    s = np.arange(SB_WIN)[None, :]
    tri = np.concatenate([(j > s).astype(np.float32), np.ones((SB_WIN, CHUNK), np.float32)], axis=1)
    return jnp.asarray(tri, BF16)


def _stickbreak(proj, mproj, seq):
    batch = proj.shape[0]
    lanes = 2 * SB_DH
    pairs = SB_HEADS // 2
    return pl.pallas_call(
        _sb_kernel,
        grid=(batch, pairs),
        in_specs=[
            pl.BlockSpec((1, seq, lanes), lambda b, p: (b, 0, OFF_SQ // lanes + p)),
            pl.BlockSpec((1, seq, lanes), lambda b, p: (b, 0, OFF_SK // lanes + p)),
            pl.BlockSpec((1, seq, lanes), lambda b, p: (b, 0, OFF_SV // lanes + p)),
            pl.BlockSpec((CHUNK, lanes), lambda b, p: (0, OFF_SK // lanes + p)),
            pl.BlockSpec((CHUNK, lanes), lambda b, p: (0, OFF_SV // lanes + p)),
            pl.BlockSpec((SB_WIN, SB_WIN + CHUNK), lambda b, p: (0, 0)),
        ],
        out_specs=pl.BlockSpec((1, seq, lanes), lambda b, p: (b, 0, p)),
        out_shape=jax.ShapeDtypeStruct((batch, seq, SB_HEADS * SB_DH), BF16),
        compiler_params=pltpu.CompilerParams(
            dimension_semantics=("parallel", "parallel"),
            vmem_limit_bytes=VMEM_LIMIT),
        name="stickbreak",
    )(proj, proj, proj, mproj, mproj, _sb_tables())


def _mix_out_kernel(yr_ref, ys_ref, ga_ref, gb_ref, x_ref, wr_ref, ws_ref, wo_ref, g_ref, o_ref):
    r = jnp.dot(yr_ref[...], wr_ref[...], preferred_element_type=F32)
    s = jnp.dot(ys_ref[...], ws_ref[...], preferred_element_type=F32)
    merged = (jax.nn.sigmoid(ga_ref[...].astype(F32)) * r
              + jax.nn.sigmoid(gb_ref[...].astype(F32)) * s)
    mix = jnp.dot(merged.astype(BF16), wo_ref[...], preferred_element_type=F32)
    o_ref[...] = x_ref[...] + _rms(mix, g_ref[...])


def _resident(shape):
    return pl.BlockSpec(shape, lambda i: (0,) * len(shape), pipeline_mode=pl.Buffered(1))


def _mix_out(yret, ysb, proj2d, x2d, wr, ws, wo, gain, tm=512):
    rows = x2d.shape[0]
    return pl.pallas_call(
        _mix_out_kernel,
        grid=(rows // tm,),
        in_specs=[
            pl.BlockSpec((tm, RET_HEADS * RET_DV), lambda i: (i, 0)),
            pl.BlockSpec((tm, D_MODEL), lambda i: (i, 0)),
            pl.BlockSpec((tm, D_MODEL), lambda i: (i, OFF_GA // D_MODEL)),
            pl.BlockSpec((tm, D_MODEL), lambda i: (i, OFF_GB // D_MODEL)),
            pl.BlockSpec((tm, D_MODEL), lambda i: (i, 0)),
            _resident(wr.shape), _resident(ws.shape), _resident(wo.shape),
            _resident((1, D_MODEL)),
        ],
        out_specs=pl.BlockSpec((tm, D_MODEL), lambda i: (i, 0)),
        out_shape=jax.ShapeDtypeStruct((rows, D_MODEL), F32),
        compiler_params=pltpu.CompilerParams(
            dimension_semantics=("parallel",),
            vmem_limit_bytes=VMEM_LIMIT),
        name="mix_out",
    )(yret, ysb, proj2d, proj2d, x2d, wr, ws, wo, gain)


FFN_COLS = D_FF // 2


def _ffn_kernel(h_ref, gpre_ref, wi_ref, wo_ref, gpost_ref, o_ref, u_ref):
    h = h_ref[...]
    hn = _rms(h, gpre_ref[...]).astype(BF16)
    for c in range(D_FF // FFN_COLS):
        lo = c * FFN_COLS
        a = jnp.dot(hn, wi_ref[:, lo:lo + FFN_COLS], preferred_element_type=F32)
        b = jnp.dot(hn, wi_ref[:, D_FF + lo:D_FF + lo + FFN_COLS], preferred_element_type=F32)
        u_ref[:, lo:lo + FFN_COLS] = (a * jax.nn.sigmoid(a) * b).astype(BF16)
    ff = jnp.dot(u_ref[...], wo_ref[...], preferred_element_type=F32)
    o_ref[...] = h + _rms(ff, gpost_ref[...])


def _ffn(h2d, gpre, wi, wo, gpost, tm=512):
    rows = h2d.shape[0]
    return pl.pallas_call(
        _ffn_kernel,
        grid=(rows // tm,),
        in_specs=[
            pl.BlockSpec((tm, D_MODEL), lambda i: (i, 0)),
            _resident((1, D_MODEL)),
            _resident(wi.shape), _resident(wo.shape),
            _resident((1, D_MODEL)),
        ],
        out_specs=pl.BlockSpec((tm, D_MODEL), lambda i: (i, 0)),
        out_shape=jax.ShapeDtypeStruct((rows, D_MODEL), F32),
        scratch_shapes=[pltpu.VMEM((tm, D_FF), BF16)],
        compiler_params=pltpu.CompilerParams(
            dimension_semantics=("parallel",),
            vmem_limit_bytes=VMEM_LIMIT),
        name="ffn",
    )(h2d, gpre, wi, wo, gpost)


def kernel(x, meta_tokens, w_in, w_ret_out, w_sb_out, w_out, w_ffn_in, w_ffn_out,
           norm_mix_pre, norm_mix_post, norm_ffn_pre, norm_ffn_post):
    batch, seq, d = x.shape
    assert d == D_MODEL and seq % 1024 == 0 and w_in.shape[0] == 1
    x2d = x.reshape(batch * seq, d)
    meta_chunk = jnp.concatenate(
        [jnp.zeros((PAD, d), x.dtype), meta_tokens.astype(x.dtype)], axis=0)

    w_in_b = w_in[0].astype(BF16)
    proj2d = _in_proj(x2d, norm_mix_pre, w_in_b, tm=1024)
    mproj = _in_proj(meta_chunk, norm_mix_pre, w_in_b, tm=CHUNK)
    proj = proj2d.reshape(batch, seq, PROJ_WIDTH)

    yret = _retention(proj, mproj, seq)
    ysb = _stickbreak(proj, mproj, seq)

    h1 = _mix_out(yret.reshape(batch * seq, -1), ysb.reshape(batch * seq, -1), proj2d, x2d,
                  w_ret_out[0].astype(BF16), w_sb_out[0].astype(BF16), w_out[0].astype(BF16),
                  norm_mix_post)
    out = _ffn(h1, norm_ffn_pre, w_ffn_in[0].astype(BF16), w_ffn_out[0].astype(BF16),
               norm_ffn_post)
    return out.reshape(batch, seq, d)
```

```python
import functools

import jax
import jax.numpy as jnp
import numpy as np
from jax import lax
from jax.experimental import pallas as pl
from jax.experimental.pallas import tpu as pltpu

D_MODEL = 1024
N_META = 16
CHUNK = 128
PAD = CHUNK - N_META
RET_HEADS = 4
RET_DK = 256
RET_DV = 512
SB_HEADS = 16
SB_DH = 64
D_FF = 2816
ROPE_BASE = 10000.0
NORM_EPS = 1e-6
GN_EPS = 1e-5

OFF_RQ = 0
OFF_RK = 1024
OFF_RV = 2048
OFF_RG = 4096
OFF_SQ = 6144
OFF_SK = 7168
OFF_SV = 8192
OFF_GA = 9216
OFF_GB = 10240
PROJ_WIDTH = 11264

F32 = jnp.float32
BF16 = jnp.bfloat16

LOG2E = 1.4426950408889634
SB_STOP = 128.0
SB_WIN = 2 * CHUNK

VMEM_LIMIT = 56 * 1024 * 1024


def _rms(xf, g):
    ms = jnp.mean(xf * xf, axis=-1, keepdims=True)
    return xf * lax.rsqrt(ms + NORM_EPS) * g


def _resident(shape):
    return pl.BlockSpec(shape, lambda i: (0,) * len(shape), pipeline_mode=pl.Buffered(1))


IN_PROJ_COLS = 1024


def _in_proj_kernel(x_ref, g_ref, w_ref, o_ref):
    hn = _rms(x_ref[...], g_ref[...]).astype(BF16)
    for j in range(PROJ_WIDTH // IN_PROJ_COLS):
        cols = slice(j * IN_PROJ_COLS, (j + 1) * IN_PROJ_COLS)
        o_ref[:, cols] = jnp.dot(hn, w_ref[:, cols], preferred_element_type=F32).astype(BF16)


def _in_proj(x2d, gain, w_bf16, tm):
    rows = x2d.shape[0]
    return pl.pallas_call(
        _in_proj_kernel,
        grid=(rows // tm,),
        in_specs=[
            pl.BlockSpec((tm, D_MODEL), lambda i: (i, 0)),
            _resident((1, D_MODEL)),
            _resident((D_MODEL, PROJ_WIDTH)),
        ],
        out_specs=pl.BlockSpec((tm, PROJ_WIDTH), lambda i: (i, 0)),
        out_shape=jax.ShapeDtypeStruct((rows, PROJ_WIDTH), BF16),
        compiler_params=pltpu.CompilerParams(
            dimension_semantics=("parallel",),
            vmem_limit_bytes=VMEM_LIMIT),
        name="in_proj",
    )(x2d, gain, w_bf16)


def _rotate(x, cos, sin):
    half = RET_DK // 2
    x1, x2 = x[:, :half], x[:, half:]
    return jnp.concatenate([x1 * cos - x2 * sin, x1 * sin + x2 * cos], axis=1)


def _kv_state(k_rot, v, zeta):
    vz = (v * zeta).astype(BF16)
    return lax.dot_general(k_rot.astype(BF16), vz, (((0,), (0,)), ((), ())),
                           preferred_element_type=F32)


def _retention_kernel(q_ref, k_ref, v_ref, g_ref, cos_ref, sin_ref,
                      mk_ref, mv_ref, mcos_ref, msin_ref,
                      decay_ref, xi_ref, zeta_ref, gchunk_ref,
                      o_ref, state_ref):
    c = pl.program_id(1)

    @pl.when(c == 0)
    def _():
        mcos, msin = mcos_ref[...], msin_ref[...]
        for h in range(RET_HEADS):
            k = mk_ref[:, h * RET_DK:(h + 1) * RET_DK].astype(F32)
            v = mv_ref[:, h * RET_DV:(h + 1) * RET_DV].astype(F32)
            state_ref[h] = _kv_state(_rotate(k, mcos, msin), v, zeta_ref[h])

    cos, sin = cos_ref[...], sin_ref[...]
    for h in range(RET_HEADS):
        q = q_ref[0, :, h * RET_DK:(h + 1) * RET_DK].astype(F32)
        k = k_ref[0, :, h * RET_DK:(h + 1) * RET_DK].astype(F32)
        v = v_ref[0, :, h * RET_DV:(h + 1) * RET_DV].astype(F32)
        q_rot = (_rotate(q, cos, sin) * (RET_DK ** -0.5)).astype(BF16)
        k_rot = _rotate(k, cos, sin)
        scores = lax.dot_general(q_rot, k_rot.astype(BF16), (((1,), (1,)), ((), ())),
                                 preferred_element_type=F32) * decay_ref[h]
        inner = jnp.dot(scores.astype(BF16), v.astype(BF16), preferred_element_type=F32)
        state = state_ref[h]
        cross = jnp.dot(q_rot, state.astype(BF16), preferred_element_type=F32) * xi_ref[h]
        state_ref[h] = gchunk_ref[h, 0:1, :] * state + _kv_state(k_rot, v, zeta_ref[h])
        y = inner + cross
        mu = jnp.mean(y, axis=-1, keepdims=True)
        yc = y - mu
        var = jnp.mean(yc * yc, axis=-1, keepdims=True)
        yn = yc * lax.rsqrt(var + GN_EPS)
        gate = g_ref[0, :, h * RET_DV:(h + 1) * RET_DV].astype(F32)
        o_ref[0, :, h * RET_DV:(h + 1) * RET_DV] = (gate * jax.nn.sigmoid(gate) * yn).astype(BF16)


def _retention_tables():
    hh = np.arange(RET_HEADS, dtype=np.float64)
    log_g = np.log1p(-(2.0 ** (-5.0 - hh)))
    idx = np.arange(CHUNK, dtype=np.float64)
    diff = idx[:, None] - idx[None, :]
    decay = np.where(diff >= 0, np.exp(log_g[:, None, None] * np.maximum(diff, 0.0)), 0.0)
    zeta = np.exp(log_g[:, None] * (CHUNK - 1.0 - idx))[:, :, None]
    xi = np.exp(log_g[:, None] * (idx + 1.0))[:, :, None]
    gchunk = np.exp(log_g * CHUNK)[:, None, None]
    bc = lambda a, shape: jnp.asarray(np.broadcast_to(a, shape), F32)
    return (bc(decay, (RET_HEADS, CHUNK, CHUNK)),
            bc(xi, (RET_HEADS, CHUNK, RET_DV)),
            bc(zeta, (RET_HEADS, CHUNK, RET_DV)),
            bc(gchunk, (RET_HEADS, 8, RET_DV)))


def _rope_tables(seq):
    half = RET_DK // 2
    inv = ROPE_BASE ** (-jnp.arange(half, dtype=F32) / half)
    pos = jnp.arange(seq + CHUNK, dtype=F32) - PAD
    ang = pos[:, None] * inv[None, :]
    cos, sin = jnp.cos(ang), jnp.sin(ang)
    return cos[CHUNK:], sin[CHUNK:], cos[:CHUNK], sin[:CHUNK]


def _retention(proj, mproj, seq):
    batch = proj.shape[0]
    n = seq // CHUNK
    cos, sin, mcos, msin = _rope_tables(seq)
    decay, xi, zeta, gchunk = _retention_tables()
    const3 = lambda b, c: (0, 0, 0)
    const2 = lambda b, c: (0, 0)
    return pl.pallas_call(
        _retention_kernel,
        grid=(batch, n),
        in_specs=[
            pl.BlockSpec((1, CHUNK, 1024), lambda b, c: (b, c, OFF_RQ // 1024)),
            pl.BlockSpec((1, CHUNK, 1024), lambda b, c: (b, c, OFF_RK // 1024)),
            pl.BlockSpec((1, CHUNK, 2048), lambda b, c: (b, c, OFF_RV // 2048)),
            pl.BlockSpec((1, CHUNK, 2048), lambda b, c: (b, c, OFF_RG // 2048)),
            pl.BlockSpec((CHUNK, RET_DK // 2), lambda b, c: (c, 0)),
            pl.BlockSpec((CHUNK, RET_DK // 2), lambda b, c: (c, 0)),
            pl.BlockSpec((CHUNK, 1024), lambda b, c: (0, OFF_RK // 1024)),
            pl.BlockSpec((CHUNK, 2048), lambda b, c: (0, OFF_RV // 2048)),
            pl.BlockSpec((CHUNK, RET_DK // 2), const2),
            pl.BlockSpec((CHUNK, RET_DK // 2), const2),
            pl.BlockSpec((RET_HEADS, CHUNK, CHUNK), const3),
            pl.BlockSpec((RET_HEADS, CHUNK, RET_DV), const3),
            pl.BlockSpec((RET_HEADS, CHUNK, RET_DV), const3),
            pl.BlockSpec((RET_HEADS, 8, RET_DV), const3),
        ],
        out_specs=pl.BlockSpec((1, CHUNK, RET_HEADS * RET_DV), lambda b, c: (b, c, 0)),
        out_shape=jax.ShapeDtypeStruct((batch, seq, RET_HEADS * RET_DV), BF16),
        scratch_shapes=[pltpu.VMEM((RET_HEADS, RET_DK, RET_DV), F32)],
        compiler_params=pltpu.CompilerParams(
            dimension_semantics=("parallel", "arbitrary"),
            vmem_limit_bytes=VMEM_LIMIT),
        name="retention",
    )(proj, proj, proj, proj, cos, sin, mproj, mproj, mcos, msin, decay, xi, zeta, gchunk)


SB_GROUP = 7


def _softplus2(z):
    return jnp.maximum(z, jnp.log2(1.0 + jnp.exp2(jnp.minimum(z, 126.0))))


def _sb_kernel(q_ref, k_ref, v_ref, mk_ref, mv_ref, tri_ref, o_ref, r_ref):
    nq = q_ref.shape[1] // CHUNK
    lanes = 2 * SB_DH
    head0_q = lax.broadcasted_iota(jnp.int32, (CHUNK, lanes), 1) < SB_DH
    head0_kv = lax.broadcasted_iota(jnp.int32, (SB_WIN, lanes), 1) < SB_DH
    t_loc = lax.broadcasted_iota(jnp.int32, (2 * CHUNK, CHUNK), 0) & (CHUNK - 1)
    s_loc = lax.broadcasted_iota(jnp.int32, (2 * CHUNK, CHUNK), 1)
    causal = s_loc < t_loc
    tri = tri_ref[...]

    def stacked_q(row0):
        qf = q_ref[0, pl.ds(row0, CHUNK), :].astype(F32) * (SB_DH ** -0.5 * LOG2E)
        both = jnp.concatenate([jnp.where(head0_q, qf, 0.0), jnp.where(head0_q, 0.0, qf)], axis=0)
        return both.astype(BF16)

    mask = lambda x, m: x if m is None else jnp.where(m, x, 0.0)

    def scores(qs, kw):
        return lax.dot_general(qs, kw, (((1,), (1,)), ((), ())), preferred_element_type=F32)

    def log_sticks(z, valid_lo, valid_hi):
        sp = _softplus2(z)
        ln = jnp.concatenate([mask(sp[:, :CHUNK], valid_lo), mask(sp[:, CHUNK:], valid_hi)], axis=1)
        return z - sp, ln[:, 0:1], ln.astype(BF16)

    def later_sums(ln):
        return jnp.dot(ln, tri, preferred_element_type=F32)

    def weights(log_beta, ln0, later, valid_lo, valid_hi, r):
        total = later[:, 0:1] + ln0
        if r is not None:
            later = later + r
            total = total + r
        a = jnp.exp2(log_beta - later)
        a = jnp.concatenate([mask(a[:, :CHUNK], valid_lo), mask(a[:, CHUNK:], valid_hi)], axis=1)
        return a.astype(BF16), total

    def attend(a, vw):
        zero = jnp.zeros_like(vw)
        return (jnp.dot(a[:CHUNK], jnp.where(head0_kv, vw, zero), preferred_element_type=F32)
                + jnp.dot(a[CHUNK:], jnp.where(head0_kv, zero, vw), preferred_element_type=F32))

    def windows(qs, kw, vw, valid_lo, valid_hi, r):
        n = len(qs)
        z = [scores(qs[u], kw[u]) for u in range(n)]
        st = [log_sticks(z[u], valid_lo, valid_hi) for u in range(n)]
        later = [later_sums(st[u][2]) for u in range(n)]
        w = [weights(st[u][0], st[u][1], later[u], valid_lo, valid_hi, r[u]) for u in range(n)]
        return [w[u][1] for u in range(n)], [attend(w[u][0], vw[u]) for u in range(n)]

    def diagonal_windows(blocks, first):
        row0 = [pl.multiple_of(i * CHUNK, CHUNK) for i in blocks]
        qs = [stacked_q(r0) for r0 in row0]
        if first:
            kw = [jnp.concatenate([mk_ref[...], k_ref[0, 0:CHUNK, :]], axis=0)]
            vw = [jnp.concatenate([mv_ref[...], v_ref[0, 0:CHUNK, :]], axis=0)]
            valid_lo = s_loc >= PAD
        else:
            start = [pl.multiple_of(r0 - CHUNK, CHUNK) for r0 in row0]
            kw = [k_ref[0, pl.ds(s, SB_WIN), :] for s in start]
            vw = [v_ref[0, pl.ds(s, SB_WIN), :] for s in start]
            valid_lo = None
        r, out = windows(qs, kw, vw, valid_lo, causal, [None] * len(blocks))
        return qs, r, out

    def load_block(ref, mref, j):
        start = pl.multiple_of(jnp.maximum(j, 0) * CHUNK, CHUNK)
        return jnp.where(j < 0, mref[...], ref[0, pl.ds(start, CHUNK), :])

    def earlier_windows(i, qs, r, acc):
        def window(state):
            j_hi, r, acc = state
            j_lo = j_hi - 1
            kw = jnp.concatenate([load_block(k_ref, mk_ref, j_lo), load_block(k_ref, mk_ref, j_hi)], axis=0)
            vw = jnp.concatenate([load_block(v_ref, mv_ref, j_lo), load_block(v_ref, mv_ref, j_hi)], axis=0)
            valid_lo = j_lo * CHUNK + s_loc >= -N_META
            valid_hi = j_hi * CHUNK + s_loc >= -N_META
            r, out = windows([qs], [kw], [vw], valid_lo, valid_hi, [r])
            return j_hi - 2, r[0], acc + out[0]

        def more(state):
            j_hi, r, _ = state
            return (j_hi >= -1) & (jnp.min(r) < SB_STOP)

        return lax.while_loop(more, window, (i - 2, r, acc))[2]

    _, _, out = diagonal_windows([0], True)
    o_ref[0, 0:CHUNK, :] = out[0].astype(BF16)

    def store(i, acc):
        o_ref[0, pl.ds(pl.multiple_of(i * CHUNK, CHUNK), CHUNK), :] = acc.astype(BF16)

    def finish_group(g):
        def block(u, carry):
            @pl.when(jnp.min(r_ref[g & 1, u]) < SB_STOP)
            def _():
                i = 1 + g * SB_GROUP + u
                qs, r, out = diagonal_windows([i], False)
                store(i, earlier_windows(i, qs[0], r[0], out[0]))
            return carry
        lax.fori_loop(0, SB_GROUP, block, 0)

    def group(g, r_prev):
        prev_unfinished = jnp.min(r_prev) < SB_STOP
        blocks = [1 + g * SB_GROUP + u for u in range(SB_GROUP)]
        _, r, out = diagonal_windows(blocks, False)
        for u, i in enumerate(blocks):
            store(i, out[u])
            r_ref[g & 1, u] = r[u]

        @pl.when(prev_unfinished)
        def _():
            finish_group(g - 1)

        return functools.reduce(jnp.minimum, r)

    assert (nq - 1) % SB_GROUP == 0
    n_groups = (nq - 1) // SB_GROUP
    r_last = lax.fori_loop(0, n_groups, group, jnp.full((2 * CHUNK, 1), SB_STOP, F32))

    @pl.when(jnp.min(r_last) < SB_STOP)
    def _():
        finish_group(n_groups - 1)


def _sb_tables():
    j = np.arange(SB_WIN)[:, None]
    s = np.arange(SB_WIN)[None, :]
    return jnp.asarray((j > s).astype(np.float32), BF16)


def _stickbreak(proj, mproj, seq):
    batch = proj.shape[0]
    lanes = 2 * SB_DH
    pairs = SB_HEADS // 2
    return pl.pallas_call(
        _sb_kernel,
        grid=(batch, pairs),
        in_specs=[
            pl.BlockSpec((1, seq, lanes), lambda b, p: (b, 0, OFF_SQ // lanes + p)),
            pl.BlockSpec((1, seq, lanes), lambda b, p: (b, 0, OFF_SK // lanes + p)),
            pl.BlockSpec((1, seq, lanes), lambda b, p: (b, 0, OFF_SV // lanes + p)),
            pl.BlockSpec((CHUNK, lanes), lambda b, p: (0, OFF_SK // lanes + p)),
            pl.BlockSpec((CHUNK, lanes), lambda b, p: (0, OFF_SV // lanes + p)),
            pl.BlockSpec((SB_WIN, SB_WIN), lambda b, p: (0, 0)),
        ],
        out_specs=pl.BlockSpec((1, seq, lanes), lambda b, p: (b, 0, p)),
        out_shape=jax.ShapeDtypeStruct((batch, seq, SB_HEADS * SB_DH), BF16),
        scratch_shapes=[pltpu.VMEM((2, SB_GROUP, 2 * CHUNK, 1), F32)],
        compiler_params=pltpu.CompilerParams(
            dimension_semantics=("parallel", "parallel"),
            vmem_limit_bytes=VMEM_LIMIT),
        name="stickbreak",
    )(proj, proj, proj, mproj, mproj, _sb_tables())


def _mix_out_kernel(yr_ref, ys_ref, ga_ref, gb_ref, x_ref, wr_ref, ws_ref, wo_ref, g_ref, o_ref):
    r = jnp.dot(yr_ref[...], wr_ref[...], preferred_element_type=F32)
    s = jnp.dot(ys_ref[...], ws_ref[...], preferred_element_type=F32)
    merged = (jax.nn.sigmoid(ga_ref[...].astype(F32)) * r
              + jax.nn.sigmoid(gb_ref[...].astype(F32)) * s)
    mix = jnp.dot(merged.astype(BF16), wo_ref[...], preferred_element_type=F32)
    o_ref[...] = x_ref[...] + _rms(mix, g_ref[...])


def _mix_out(yret, ysb, proj2d, x2d, wr, ws, wo, gain, tm=512):
    rows = x2d.shape[0]
    return pl.pallas_call(
        _mix_out_kernel,
        grid=(rows // tm,),
        in_specs=[
            pl.BlockSpec((tm, RET_HEADS * RET_DV), lambda i: (i, 0)),
            pl.BlockSpec((tm, D_MODEL), lambda i: (i, 0)),
            pl.BlockSpec((tm, D_MODEL), lambda i: (i, OFF_GA // D_MODEL)),
            pl.BlockSpec((tm, D_MODEL), lambda i: (i, OFF_GB // D_MODEL)),
            pl.BlockSpec((tm, D_MODEL), lambda i: (i, 0)),
            _resident(wr.shape), _resident(ws.shape), _resident(wo.shape),
            _resident((1, D_MODEL)),
        ],
        out_specs=pl.BlockSpec((tm, D_MODEL), lambda i: (i, 0)),
        out_shape=jax.ShapeDtypeStruct((rows, D_MODEL), F32),
        compiler_params=pltpu.CompilerParams(
            dimension_semantics=("parallel",),
            vmem_limit_bytes=VMEM_LIMIT),
        name="mix_out",
    )(yret, ysb, proj2d, proj2d, x2d, wr, ws, wo, gain)


FFN_COLS = D_FF // 2


def _ffn_kernel(h_ref, gpre_ref, wi_ref, wo_ref, gpost_ref, o_ref, u_ref):
    h = h_ref[...]
    hn = _rms(h, gpre_ref[...]).astype(BF16)
    for c in range(D_FF // FFN_COLS):
        lo = c * FFN_COLS
        a = jnp.dot(hn, wi_ref[:, lo:lo + FFN_COLS], preferred_element_type=F32)
        b = jnp.dot(hn, wi_ref[:, D_FF + lo:D_FF + lo + FFN_COLS], preferred_element_type=F32)
        u_ref[:, lo:lo + FFN_COLS] = (a * jax.nn.sigmoid(a) * b).astype(BF16)
    ff = jnp.dot(u_ref[...], wo_ref[...], preferred_element_type=F32)
    o_ref[...] = h + _rms(ff, gpost_ref[...])


def _ffn(h2d, gpre, wi, wo, gpost, tm=512):
    rows = h2d.shape[0]
    return pl.pallas_call(
        _ffn_kernel,
        grid=(rows // tm,),
        in_specs=[
            pl.BlockSpec((tm, D_MODEL), lambda i: (i, 0)),
            _resident((1, D_MODEL)),
            _resident(wi.shape), _resident(wo.shape),
            _resident((1, D_MODEL)),
        ],
        out_specs=pl.BlockSpec((tm, D_MODEL), lambda i: (i, 0)),
        out_shape=jax.ShapeDtypeStruct((rows, D_MODEL), F32),
        scratch_shapes=[pltpu.VMEM((tm, D_FF), BF16)],
        compiler_params=pltpu.CompilerParams(
            dimension_semantics=("parallel",),
            vmem_limit_bytes=VMEM_LIMIT),
        name="ffn",
    )(h2d, gpre, wi, wo, gpost)


def kernel(x, meta_tokens, w_in, w_ret_out, w_sb_out, w_out, w_ffn_in, w_ffn_out,
           norm_mix_pre, norm_mix_post, norm_ffn_pre, norm_ffn_post):
    batch, seq, d = x.shape
    assert d == D_MODEL and seq % 1024 == 0 and w_in.shape[0] == 1
    x2d = x.reshape(batch * seq, d)
    meta_chunk = jnp.concatenate(
        [jnp.zeros((PAD, d), x.dtype), meta_tokens.astype(x.dtype)], axis=0)

    w_in_b = w_in[0].astype(BF16)
    proj2d = _in_proj(x2d, norm_mix_pre, w_in_b, tm=512)
    mproj = _in_proj(meta_chunk, norm_mix_pre, w_in_b, tm=CHUNK)
    proj = proj2d.reshape(batch, seq, PROJ_WIDTH)

    yret = _retention(proj, mproj, seq)
    ysb = _stickbreak(proj, mproj, seq)

    h1 = _mix_out(yret.reshape(batch * seq, -1), ysb.reshape(batch * seq, -1), proj2d, x2d,
                  w_ret_out[0].astype(BF16), w_sb_out[0].astype(BF16), w_out[0].astype(BF16),
                  norm_mix_post)
    out = _ffn(h1, norm_ffn_pre, w_ffn_in[0].astype(BF16), w_ffn_out[0].astype(BF16),
               norm_ffn_post)
    return out.reshape(batch, seq, d)
```

```python
import functools

import jax
import jax.numpy as jnp
import numpy as np
from jax import lax
from jax.experimental import pallas as pl
from jax.experimental.pallas import tpu as pltpu

D_MODEL = 1024
N_META = 16
CHUNK = 128
PAD = CHUNK - N_META
RET_HEADS = 4
RET_DK = 256
RET_DV = 512
SB_HEADS = 16
SB_DH = 64
D_FF = 2816
ROPE_BASE = 10000.0
NORM_EPS = 1e-6
GN_EPS = 1e-5

OFF_RQ = 0
OFF_RK = 1024
OFF_RV = 2048
OFF_RG = 4096
OFF_SQ = 6144
OFF_SK = 7168
OFF_SV = 8192
OFF_GA = 9216
OFF_GB = 10240
PROJ_WIDTH = 11264

F32 = jnp.float32
BF16 = jnp.bfloat16

LOG2E = 1.4426950408889634
SB_STOP = 128.0
SB_WIN = 2 * CHUNK

VMEM_LIMIT = 60 * 1024 * 1024


def _rms(xf, g):
    ms = jnp.mean(xf * xf, axis=-1, keepdims=True)
    return xf * lax.rsqrt(ms + NORM_EPS) * g


def _resident(shape):
    return pl.BlockSpec(shape, lambda i: (0,) * len(shape), pipeline_mode=pl.Buffered(1))


IN_PROJ_COLS = 1024


def _in_proj_kernel(x_ref, g_ref, w_ref, o_ref):
    hn = _rms(x_ref[...], g_ref[...]).astype(BF16)
    for j in range(PROJ_WIDTH // IN_PROJ_COLS):
        cols = slice(j * IN_PROJ_COLS, (j + 1) * IN_PROJ_COLS)
        o_ref[:, cols] = jnp.dot(hn, w_ref[:, cols], preferred_element_type=F32).astype(BF16)


def _in_proj(x2d, gain, w_bf16, tm):
    rows = x2d.shape[0]
    return pl.pallas_call(
        _in_proj_kernel,
        grid=(rows // tm,),
        in_specs=[
            pl.BlockSpec((tm, D_MODEL), lambda i: (i, 0)),
            _resident((1, D_MODEL)),
            _resident((D_MODEL, PROJ_WIDTH)),
        ],
        out_specs=pl.BlockSpec((tm, PROJ_WIDTH), lambda i: (i, 0)),
        out_shape=jax.ShapeDtypeStruct((rows, PROJ_WIDTH), BF16),
        compiler_params=pltpu.CompilerParams(
            dimension_semantics=("parallel",),
            vmem_limit_bytes=VMEM_LIMIT),
        name="in_proj",
    )(x2d, gain, w_bf16)


def _rotate(x, cos, sin):
    half = RET_DK // 2
    x1, x2 = x[:, :half], x[:, half:]
    return jnp.concatenate([x1 * cos - x2 * sin, x1 * sin + x2 * cos], axis=1)


def _kv_state(k_rot, v, zeta):
    vz = (v * zeta).astype(BF16)
    return lax.dot_general(k_rot.astype(BF16), vz, (((0,), (0,)), ((), ())),
                           preferred_element_type=F32)


def _retention_kernel(q_ref, k_ref, v_ref, g_ref, cbase_ref, sbase_ref, coff_ref, soff_ref,
                      mk_ref, mv_ref, mcos_ref, msin_ref,
                      decay_ref, xi_ref, zeta_ref, gchunk_ref,
                      o_ref, state_ref):
    c = pl.program_id(1)

    @pl.when(c == 0)
    def _():
        mcos, msin = mcos_ref[...], msin_ref[...]
        for h in range(RET_HEADS):
            k = mk_ref[:, h * RET_DK:(h + 1) * RET_DK].astype(F32)
            v = mv_ref[:, h * RET_DV:(h + 1) * RET_DV].astype(F32)
            state_ref[h] = _kv_state(_rotate(k, mcos, msin), v, zeta_ref[h])

    cb, sb, co, so = cbase_ref[0], sbase_ref[0], coff_ref[...], soff_ref[...]
    cos = cb * co - sb * so
    sin = sb * co + cb * so
    for h in range(RET_HEADS):
        q = q_ref[0, :, h * RET_DK:(h + 1) * RET_DK].astype(F32)
        k = k_ref[0, :, h * RET_DK:(h + 1) * RET_DK].astype(F32)
        v = v_ref[0, :, h * RET_DV:(h + 1) * RET_DV].astype(F32)
        q_rot = (_rotate(q, cos, sin) * (RET_DK ** -0.5)).astype(BF16)
        k_rot = _rotate(k, cos, sin)
        scores = lax.dot_general(q_rot, k_rot.astype(BF16), (((1,), (1,)), ((), ())),
                                 preferred_element_type=F32) * decay_ref[h]
        inner = jnp.dot(scores.astype(BF16), v.astype(BF16), preferred_element_type=F32)
        state = state_ref[h]
        cross = jnp.dot(q_rot, state.astype(BF16), preferred_element_type=F32) * xi_ref[h]
        state_ref[h] = gchunk_ref[h, 0:1, :] * state + _kv_state(k_rot, v, zeta_ref[h])
        y = inner + cross
        mu = jnp.mean(y, axis=-1, keepdims=True)
        yc = y - mu
        var = jnp.mean(yc * yc, axis=-1, keepdims=True)
        yn = yc * lax.rsqrt(var + GN_EPS)
        gate = g_ref[0, :, h * RET_DV:(h + 1) * RET_DV].astype(F32)
        o_ref[0, :, h * RET_DV:(h + 1) * RET_DV] = (gate * jax.nn.sigmoid(gate) * yn).astype(BF16)


def _retention_tables():
    hh = np.arange(RET_HEADS, dtype=np.float64)
    log_g = np.log1p(-(2.0 ** (-5.0 - hh)))
    idx = np.arange(CHUNK, dtype=np.float64)
    diff = idx[:, None] - idx[None, :]
    decay = np.where(diff >= 0, np.exp(log_g[:, None, None] * np.maximum(diff, 0.0)), 0.0)
    zeta = np.exp(log_g[:, None] * (CHUNK - 1.0 - idx))[:, :, None]
    xi = np.exp(log_g[:, None] * (idx + 1.0))[:, :, None]
    gchunk = np.exp(log_g * CHUNK)[:, None, None]
    bc = lambda a, shape: jnp.asarray(np.broadcast_to(a, shape), F32)
    return (bc(decay, (RET_HEADS, CHUNK, CHUNK)),
            bc(xi, (RET_HEADS, CHUNK, RET_DV)),
            bc(zeta, (RET_HEADS, CHUNK, RET_DV)),
            bc(gchunk, (RET_HEADS, 8, RET_DV)))


def _rope_tables(seq):
    half = RET_DK // 2
    inv = ROPE_BASE ** (-np.arange(half, dtype=np.float64) / half)
    base = (N_META + CHUNK * np.arange(seq // CHUNK, dtype=np.float64))[:, None, None] * inv
    off = np.arange(CHUNK, dtype=np.float64)[:, None] * inv
    meta = (np.arange(CHUNK, dtype=np.float64) - PAD)[:, None] * inv
    f = lambda a: jnp.asarray(a, F32)
    return (f(np.cos(base)), f(np.sin(base)), f(np.cos(off)), f(np.sin(off)),
            f(np.cos(meta)), f(np.sin(meta)))


def _retention(proj, mproj, seq):
    batch = proj.shape[0]
    n = seq // CHUNK
    cbase, sbase, coff, soff, mcos, msin = _rope_tables(seq)
    decay, xi, zeta, gchunk = _retention_tables()
    const3 = lambda b, c: (0, 0, 0)
    const2 = lambda b, c: (0, 0)
    return pl.pallas_call(
        _retention_kernel,
        grid=(batch, n),
        in_specs=[
            pl.BlockSpec((1, CHUNK, 1024), lambda b, c: (b, c, OFF_RQ // 1024)),
            pl.BlockSpec((1, CHUNK, 1024), lambda b, c: (b, c, OFF_RK // 1024)),
            pl.BlockSpec((1, CHUNK, 2048), lambda b, c: (b, c, OFF_RV // 2048)),
            pl.BlockSpec((1, CHUNK, 2048), lambda b, c: (b, c, OFF_RG // 2048)),
            pl.BlockSpec((1, 1, RET_DK // 2), lambda b, c: (c, 0, 0)),
            pl.BlockSpec((1, 1, RET_DK // 2), lambda b, c: (c, 0, 0)),
            pl.BlockSpec((CHUNK, RET_DK // 2), const2),
            pl.BlockSpec((CHUNK, RET_DK // 2), const2),
            pl.BlockSpec((CHUNK, 1024), lambda b, c: (0, OFF_RK // 1024)),
            pl.BlockSpec((CHUNK, 2048), lambda b, c: (0, OFF_RV // 2048)),
            pl.BlockSpec((CHUNK, RET_DK // 2), const2),
            pl.BlockSpec((CHUNK, RET_DK // 2), const2),
            pl.BlockSpec((RET_HEADS, CHUNK, CHUNK), const3),
            pl.BlockSpec((RET_HEADS, CHUNK, RET_DV), const3),
            pl.BlockSpec((RET_HEADS, CHUNK, RET_DV), const3),
            pl.BlockSpec((RET_HEADS, 8, RET_DV), const3),
        ],
        out_specs=pl.BlockSpec((1, CHUNK, RET_HEADS * RET_DV), lambda b, c: (b, c, 0)),
        out_shape=jax.ShapeDtypeStruct((batch, seq, RET_HEADS * RET_DV), BF16),
        scratch_shapes=[pltpu.VMEM((RET_HEADS, RET_DK, RET_DV), F32)],
        compiler_params=pltpu.CompilerParams(
            dimension_semantics=("parallel", "arbitrary"),
            vmem_limit_bytes=VMEM_LIMIT),
        name="retention",
    )(proj, proj, proj, proj, cbase, sbase, coff, soff, mproj, mproj, mcos, msin, decay, xi, zeta, gchunk)


SB_GROUP = 7


def _softplus2(z):
    return jnp.maximum(z, jnp.log2(1.0 + jnp.exp2(jnp.minimum(z, 126.0))))


def _sb_kernel(qkv_ref, mqkv_ref, tri_ref, o_ref, r_ref):
    nq = qkv_ref.shape[1] // CHUNK
    lanes = 2 * SB_DH
    q_rows = lambda start, size: qkv_ref[0, pl.ds(start, size), 0:lanes]
    k_rows = lambda start, size: qkv_ref[0, pl.ds(start, size), lanes:2 * lanes]
    v_rows = lambda start, size: qkv_ref[0, pl.ds(start, size), 2 * lanes:3 * lanes]
    meta_k = lambda: mqkv_ref[:, lanes:2 * lanes]
    meta_v = lambda: mqkv_ref[:, 2 * lanes:3 * lanes]
    head0_q = lax.broadcasted_iota(jnp.int32, (CHUNK, lanes), 1) < SB_DH
    head0_kv = lax.broadcasted_iota(jnp.int32, (SB_WIN, lanes), 1) < SB_DH
    t_loc = lax.broadcasted_iota(jnp.int32, (2 * CHUNK, CHUNK), 0) & (CHUNK - 1)
    s_loc = lax.broadcasted_iota(jnp.int32, (2 * CHUNK, CHUNK), 1)
    causal = s_loc < t_loc
    tri = tri_ref[...]

    def stacked_q(row0):
        qf = q_rows(row0, CHUNK).astype(F32) * (SB_DH ** -0.5 * LOG2E)
        both = jnp.concatenate([jnp.where(head0_q, qf, 0.0), jnp.where(head0_q, 0.0, qf)], axis=0)
        return both.astype(BF16)

    mask = lambda x, m: x if m is None else jnp.where(m, x, 0.0)

    def scores(qs, kw):
        return lax.dot_general(qs, kw, (((1,), (1,)), ((), ())), preferred_element_type=F32)

    def log_sticks(z, valid_lo, valid_hi):
        sp = _softplus2(z)
        ln = jnp.concatenate([mask(sp[:, :CHUNK], valid_lo), mask(sp[:, CHUNK:], valid_hi)], axis=1)
        return z - sp, ln[:, 0:1], ln.astype(BF16)

    def later_sums(ln):
        return jnp.dot(ln, tri, preferred_element_type=F32)

    def weights(log_beta, ln0, later, valid_lo, valid_hi, r):
        total = later[:, 0:1] + ln0
        if r is not None:
            later = later + r
            total = total + r
        a = jnp.exp2(log_beta - later)
        a = jnp.concatenate([mask(a[:, :CHUNK], valid_lo), mask(a[:, CHUNK:], valid_hi)], axis=1)
        return a.astype(BF16), total

    def attend(a, vw):
        zero = jnp.zeros_like(vw)
        return (jnp.dot(a[:CHUNK], jnp.where(head0_kv, vw, zero), preferred_element_type=F32)
                + jnp.dot(a[CHUNK:], jnp.where(head0_kv, zero, vw), preferred_element_type=F32))

    def windows(qs, kw, vw, valid_lo, valid_hi, r):
        n = len(qs)
        z = [scores(qs[u], kw[u]) for u in range(n)]
        st = [log_sticks(z[u], valid_lo, valid_hi) for u in range(n)]
        later = [later_sums(st[u][2]) for u in range(n)]
        w = [weights(st[u][0], st[u][1], later[u], valid_lo, valid_hi, r[u]) for u in range(n)]
        return [w[u][1] for u in range(n)], [attend(w[u][0], vw[u]) for u in range(n)]

    def diagonal_windows(blocks, first):
        row0 = [pl.multiple_of(i * CHUNK, CHUNK) for i in blocks]
        qs = [stacked_q(r0) for r0 in row0]
        if first:
            kw = [jnp.concatenate([meta_k(), k_rows(0, CHUNK)], axis=0)]
            vw = [jnp.concatenate([meta_v(), v_rows(0, CHUNK)], axis=0)]
            valid_lo = s_loc >= PAD
        else:
            start = [pl.multiple_of(r0 - CHUNK, CHUNK) for r0 in row0]
            kw = [k_rows(s, SB_WIN) for s in start]
            vw = [v_rows(s, SB_WIN) for s in start]
            valid_lo = None
        r, out = windows(qs, kw, vw, valid_lo, causal, [None] * len(blocks))
        return qs, r, out

    def load_block(rows, meta, j):
        start = pl.multiple_of(jnp.maximum(j, 0) * CHUNK, CHUNK)
        return jnp.where(j < 0, meta(), rows(start, CHUNK))

    def earlier_windows(i, qs, r, acc):
        def window(state):
            j_hi, r, acc = state
            j_lo = j_hi - 1
            kw = jnp.concatenate([load_block(k_rows, meta_k, j_lo), load_block(k_rows, meta_k, j_hi)], axis=0)
            vw = jnp.concatenate([load_block(v_rows, meta_v, j_lo), load_block(v_rows, meta_v, j_hi)], axis=0)
            valid_lo = j_lo * CHUNK + s_loc >= -N_META
            valid_hi = j_hi * CHUNK + s_loc >= -N_META
            r, out = windows([qs], [kw], [vw], valid_lo, valid_hi, [r])
            return j_hi - 2, r[0], acc + out[0]

        def more(state):
            j_hi, r, _ = state
            return (j_hi >= -1) & (jnp.min(r) < SB_STOP)

        return lax.while_loop(more, window, (i - 2, r, acc))[2]

    _, _, out = diagonal_windows([0], True)
    o_ref[0, 0:CHUNK, :] = out[0].astype(BF16)

    def store(i, acc):
        o_ref[0, pl.ds(pl.multiple_of(i * CHUNK, CHUNK), CHUNK), :] = acc.astype(BF16)

    def finish_group(g):
        def block(u, carry):
            @pl.when(jnp.min(r_ref[g & 1, u]) < SB_STOP)
            def _():
                i = 1 + g * SB_GROUP + u
                qs, r, out = diagonal_windows([i], False)
                store(i, earlier_windows(i, qs[0], r[0], out[0]))
            return carry
        lax.fori_loop(0, SB_GROUP, block, 0)

    def group(g, r_prev):
        prev_unfinished = jnp.min(r_prev) < SB_STOP
        blocks = [1 + g * SB_GROUP + u for u in range(SB_GROUP)]
        _, r, out = diagonal_windows(blocks, False)
        for u, i in enumerate(blocks):
            store(i, out[u])
            r_ref[g & 1, u] = r[u]

        @pl.when(prev_unfinished)
        def _():
            finish_group(g - 1)

        return functools.reduce(jnp.minimum, r)

    assert (nq - 1) % SB_GROUP == 0
    n_groups = (nq - 1) // SB_GROUP
    r_last = lax.fori_loop(0, n_groups, group, jnp.full((2 * CHUNK, 1), SB_STOP, F32))

    @pl.when(jnp.min(r_last) < SB_STOP)
    def _():
        finish_group(n_groups - 1)


def _sb_tables():
    j = np.arange(SB_WIN)[:, None]
    s = np.arange(SB_WIN)[None, :]
    return jnp.asarray((j > s).astype(np.float32), BF16)


def _stickbreak(proj, mproj, seq):
    batch = proj.shape[0]
    lanes = 2 * SB_DH
    width = 3 * lanes
    pairs = SB_HEADS // 2
    return pl.pallas_call(
        _sb_kernel,
        grid=(batch, pairs),
        in_specs=[
            pl.BlockSpec((1, seq, width), lambda b, p: (b, 0, OFF_SQ // width + p)),
            pl.BlockSpec((CHUNK, width), lambda b, p: (0, OFF_SQ // width + p)),
            pl.BlockSpec((SB_WIN, SB_WIN), lambda b, p: (0, 0)),
        ],
        out_specs=pl.BlockSpec((1, seq, lanes), lambda b, p: (b, 0, p)),
        out_shape=jax.ShapeDtypeStruct((batch, seq, SB_HEADS * SB_DH), BF16),
        scratch_shapes=[pltpu.VMEM((2, SB_GROUP, 2 * CHUNK, 1), F32)],
        compiler_params=pltpu.CompilerParams(
            dimension_semantics=("parallel", "parallel"),
            vmem_limit_bytes=VMEM_LIMIT),
        name="stickbreak",
    )(proj, mproj, _sb_tables())


def _pair_major(w_in):
    sb = w_in[:, OFF_SQ:OFF_GA].reshape(D_MODEL, 3, SB_HEADS // 2, 2 * SB_DH)
    sb = sb.transpose(0, 2, 1, 3).reshape(D_MODEL, OFF_GA - OFF_SQ)
    return jnp.concatenate([w_in[:, :OFF_SQ], sb, w_in[:, OFF_GA:]], axis=1)


FFN_COLS = 256


def _post_kernel(yr_ref, ys_ref, ga_ref, gb_ref, x_ref, wr_ref, ws_ref, wo_ref, gmix_ref,
                 gpre_ref, wi_ref, wf_ref, gpost_ref, o_ref, u_ref):
    r = jnp.dot(yr_ref[...], wr_ref[...], preferred_element_type=F32)
    s = jnp.dot(ys_ref[...], ws_ref[...], preferred_element_type=F32)
    merged = (jax.nn.sigmoid(ga_ref[...].astype(F32)) * r
              + jax.nn.sigmoid(gb_ref[...].astype(F32)) * s)
    mix = jnp.dot(merged.astype(BF16), wo_ref[...], preferred_element_type=F32)
    h = x_ref[...] + _rms(mix, gmix_ref[...])

    hn = _rms(h, gpre_ref[...]).astype(BF16)
    for c in range(D_FF // FFN_COLS):
        lo = c * FFN_COLS
        a = jnp.dot(hn, wi_ref[:, lo:lo + FFN_COLS], preferred_element_type=F32)
        b = jnp.dot(hn, wi_ref[:, D_FF + lo:D_FF + lo + FFN_COLS], preferred_element_type=F32)
        u_ref[:, lo:lo + FFN_COLS] = (a * jax.nn.sigmoid(a) * b).astype(BF16)
    ff = jnp.dot(u_ref[...], wf_ref[...], preferred_element_type=F32)
    o_ref[...] = h + _rms(ff, gpost_ref[...])


def _post(yret, ysb, proj2d, x2d, wr, ws, wo, gmix, gpre, wi, wf, gpost, tm=512):
    rows = x2d.shape[0]
    row_block = lambda width, col=0: pl.BlockSpec((tm, width), lambda i: (i, col))
    return pl.pallas_call(
        _post_kernel,
        grid=(rows // tm,),
        in_specs=[
            row_block(RET_HEADS * RET_DV),
            row_block(D_MODEL),
            row_block(D_MODEL, OFF_GA // D_MODEL),
            row_block(D_MODEL, OFF_GB // D_MODEL),
            row_block(D_MODEL),
            _resident(wr.shape), _resident(ws.shape), _resident(wo.shape),
            _resident((1, D_MODEL)), _resident((1, D_MODEL)),
            _resident(wi.shape), _resident(wf.shape),
            _resident((1, D_MODEL)),
        ],
        out_specs=row_block(D_MODEL),
        out_shape=jax.ShapeDtypeStruct((rows, D_MODEL), F32),
        scratch_shapes=[pltpu.VMEM((tm, D_FF), BF16)],
        compiler_params=pltpu.CompilerParams(
            dimension_semantics=("parallel",),
            vmem_limit_bytes=VMEM_LIMIT),
        name="post",
    )(yret, ysb, proj2d, proj2d, x2d, wr, ws, wo, gmix, gpre, wi, wf, gpost)


def kernel(x, meta_tokens, w_in, w_ret_out, w_sb_out, w_out, w_ffn_in, w_ffn_out,
           norm_mix_pre, norm_mix_post, norm_ffn_pre, norm_ffn_post):
    batch, seq, d = x.shape
    assert d == D_MODEL and seq % 1024 == 0 and w_in.shape[0] == 1
    x2d = x.reshape(batch * seq, d)
    meta_chunk = jnp.concatenate(
        [jnp.zeros((PAD, d), x.dtype), meta_tokens.astype(x.dtype)], axis=0)

    w_in_b = _pair_major(w_in[0]).astype(BF16)
    proj2d = _in_proj(x2d, norm_mix_pre, w_in_b, tm=512)
    mproj = _in_proj(meta_chunk, norm_mix_pre, w_in_b, tm=CHUNK)
    proj = proj2d.reshape(batch, seq, PROJ_WIDTH)

    yret = _retention(proj, mproj, seq)
    ysb = _stickbreak(proj, mproj, seq)

    bf = lambda w: w[0].astype(BF16)
    out = _post(yret.reshape(batch * seq, -1), ysb.reshape(batch * seq, -1), proj2d, x2d,
                bf(w_ret_out), bf(w_sb_out), bf(w_out), norm_mix_post,
                norm_ffn_pre, bf(w_ffn_in), bf(w_ffn_out), norm_ffn_post)
    return out.reshape(batch, seq, d)
```

```python
import functools

import jax
import jax.numpy as jnp
import numpy as np
from jax import lax
from jax.experimental import pallas as pl
from jax.experimental.pallas import tpu as pltpu

D_MODEL = 1024
N_META = 16
CHUNK = 128
PAD = CHUNK - N_META
RET_HEADS = 4
RET_DK = 256
RET_DV = 512
SB_HEADS = 16
SB_DH = 64
D_FF = 2816
ROPE_BASE = 10000.0
NORM_EPS = 1e-6
GN_EPS = 1e-5

OFF_RQ = 0
OFF_RK = 1024
OFF_RV = 2048
OFF_RG = 4096
OFF_SQ = 6144
OFF_SK = 7168
OFF_SV = 8192
OFF_GA = 9216
OFF_GB = 10240
PROJ_WIDTH = 11264

F32 = jnp.float32
BF16 = jnp.bfloat16

LOG2E = 1.4426950408889634
SB_STOP = 127.0
SB_WIN = 2 * CHUNK

VMEM_LIMIT = 60 * 1024 * 1024


def _rms(xf, g):
    ms = jnp.mean(xf * xf, axis=-1, keepdims=True)
    return xf * lax.rsqrt(ms + NORM_EPS) * g


def _resident(shape):
    return pl.BlockSpec(shape, lambda i: (0,) * len(shape), pipeline_mode=pl.Buffered(1))


IN_PROJ_COLS = 1024


def _in_proj_kernel(x_ref, m_ref, g_ref, w_ref, o_ref, mo_ref):
    def project(rows, out_ref):
        hn = _rms(rows, g_ref[...]).astype(BF16)
        for j in range(PROJ_WIDTH // IN_PROJ_COLS):
            cols = slice(j * IN_PROJ_COLS, (j + 1) * IN_PROJ_COLS)
            out_ref[:, cols] = jnp.dot(hn, w_ref[:, cols], preferred_element_type=F32).astype(BF16)

    project(x_ref[...], o_ref)

    @pl.when(pl.program_id(0) == 0)
    def _():
        project(m_ref[...], mo_ref)


def _in_proj(x2d, meta_chunk, gain, w_bf16, tm):
    rows = x2d.shape[0]
    return pl.pallas_call(
        _in_proj_kernel,
        grid=(rows // tm,),
        in_specs=[
            pl.BlockSpec((tm, D_MODEL), lambda i: (i, 0)),
            _resident((CHUNK, D_MODEL)),
            _resident((1, D_MODEL)),
            _resident((D_MODEL, PROJ_WIDTH)),
        ],
        out_specs=[
            pl.BlockSpec((tm, PROJ_WIDTH), lambda i: (i, 0)),
            pl.BlockSpec((CHUNK, PROJ_WIDTH), lambda i: (0, 0)),
        ],
        out_shape=[
            jax.ShapeDtypeStruct((rows, PROJ_WIDTH), BF16),
            jax.ShapeDtypeStruct((CHUNK, PROJ_WIDTH), BF16),
        ],
        compiler_params=pltpu.CompilerParams(
            dimension_semantics=("arbitrary",),
            vmem_limit_bytes=VMEM_LIMIT),
        name="in_proj",
    )(x2d, meta_chunk, gain, w_bf16)


def _rotate(x, cos, sin):
    half = RET_DK // 2
    x1, x2 = x[:, :half], x[:, half:]
    return jnp.concatenate([x1 * cos - x2 * sin, x1 * sin + x2 * cos], axis=1)


RET_ROWS = 512


def _kv_state(k_rot, v, zeta):
    kz = (k_rot * zeta).astype(BF16)
    return lax.dot_general(kz, v, (((0,), (0,)), ((), ())), preferred_element_type=F32)


def _retention_kernel(q_ref, k_ref, v_ref, g_ref, cbase_ref, sbase_ref, coff_ref, soff_ref,
                      mk_ref, mv_ref, mcos_ref, msin_ref,
                      decay_ref, xi_ref, zeta_ref, gchunk_ref,
                      o_ref, state_ref):
    @pl.when(pl.program_id(1) == 0)
    def _():
        mcos, msin = mcos_ref[...], msin_ref[...]
        for h in range(RET_HEADS):
            k = mk_ref[:, h * RET_DK:(h + 1) * RET_DK].astype(F32)
            v = mv_ref[:, h * RET_DV:(h + 1) * RET_DV]
            state_ref[h] = _kv_state(_rotate(k, mcos, msin), v, zeta_ref[h])

    co, so = coff_ref[...], soff_ref[...]
    for cc in range(RET_ROWS // CHUNK):
        rows = slice(cc * CHUNK, (cc + 1) * CHUNK)
        cb, sb = cbase_ref[cc], sbase_ref[cc]
        cos = cb * co - sb * so
        sin = sb * co + cb * so
        for h in range(RET_HEADS):
            q = q_ref[0, rows, h * RET_DK:(h + 1) * RET_DK].astype(F32)
            k = k_ref[0, rows, h * RET_DK:(h + 1) * RET_DK].astype(F32)
            v = v_ref[0, rows, h * RET_DV:(h + 1) * RET_DV]
            q_rot = (_rotate(q, cos, sin) * (RET_DK ** -0.5)).astype(BF16)
            k_rot = _rotate(k, cos, sin)
            scores = lax.dot_general(q_rot, k_rot.astype(BF16), (((1,), (1,)), ((), ())),
                                     preferred_element_type=F32) * decay_ref[h]
            inner = jnp.dot(scores.astype(BF16), v, preferred_element_type=F32)
            state = state_ref[h]
            cross = jnp.dot(q_rot, state.astype(BF16), preferred_element_type=F32) * xi_ref[h]
            state_ref[h] = gchunk_ref[h, 0:1, :] * state + _kv_state(k_rot, v, zeta_ref[h])
            y = inner + cross
            mu = jnp.mean(y, axis=-1, keepdims=True)
            yc = y - mu
            var = jnp.mean(yc * yc, axis=-1, keepdims=True)
            yn = yc * lax.rsqrt(var + GN_EPS)
            gate = g_ref[0, rows, h * RET_DV:(h + 1) * RET_DV].astype(F32)
            o_ref[0, rows, h * RET_DV:(h + 1) * RET_DV] = (gate * jax.nn.sigmoid(gate) * yn).astype(BF16)


def _retention_tables():
    hh = np.arange(RET_HEADS, dtype=np.float64)
    log_g = np.log1p(-(2.0 ** (-5.0 - hh)))
    idx = np.arange(CHUNK, dtype=np.float64)
    diff = idx[:, None] - idx[None, :]
    decay = np.where(diff >= 0, np.exp(log_g[:, None, None] * np.maximum(diff, 0.0)), 0.0)
    zeta = np.exp(log_g[:, None] * (CHUNK - 1.0 - idx))[:, :, None]
    xi = np.exp(log_g[:, None] * (idx + 1.0))[:, :, None]
    gchunk = np.exp(log_g * CHUNK)[:, None, None]
    bc = lambda a, shape: jnp.asarray(np.broadcast_to(a, shape), F32)
    return (bc(decay, (RET_HEADS, CHUNK, CHUNK)),
            bc(xi, (RET_HEADS, CHUNK, RET_DV)),
            bc(zeta, (RET_HEADS, CHUNK, RET_DK)),
            bc(gchunk, (RET_HEADS, 8, RET_DV)))


def _rope_tables(seq):
    half = RET_DK // 2
    inv = ROPE_BASE ** (-np.arange(half, dtype=np.float64) / half)
    base = (N_META + CHUNK * np.arange(seq // CHUNK, dtype=np.float64))[:, None, None] * inv
    off = np.arange(CHUNK, dtype=np.float64)[:, None] * inv
    meta = (np.arange(CHUNK, dtype=np.float64) - PAD)[:, None] * inv
    f = lambda a: jnp.asarray(a, F32)
    return (f(np.cos(base)), f(np.sin(base)), f(np.cos(off)), f(np.sin(off)),
            f(np.cos(meta)), f(np.sin(meta)))


def _retention(proj, mproj, seq):
    batch = proj.shape[0]
    steps = seq // RET_ROWS
    per_step = RET_ROWS // CHUNK
    cbase, sbase, coff, soff, mcos, msin = _rope_tables(seq)
    decay, xi, zeta, gchunk = _retention_tables()
    const3 = lambda b, c: (0, 0, 0)
    const2 = lambda b, c: (0, 0)
    return pl.pallas_call(
        _retention_kernel,
        grid=(batch, steps),
        in_specs=[
            pl.BlockSpec((1, RET_ROWS, 1024), lambda b, c: (b, c, OFF_RQ // 1024)),
            pl.BlockSpec((1, RET_ROWS, 1024), lambda b, c: (b, c, OFF_RK // 1024)),
            pl.BlockSpec((1, RET_ROWS, 2048), lambda b, c: (b, c, OFF_RV // 2048)),
            pl.BlockSpec((1, RET_ROWS, 2048), lambda b, c: (b, c, OFF_RG // 2048)),
            pl.BlockSpec((per_step, 1, RET_DK // 2), lambda b, c: (c, 0, 0)),
            pl.BlockSpec((per_step, 1, RET_DK // 2), lambda b, c: (c, 0, 0)),
            pl.BlockSpec((CHUNK, RET_DK // 2), const2),
            pl.BlockSpec((CHUNK, RET_DK // 2), const2),
            pl.BlockSpec((CHUNK, 1024), lambda b, c: (0, OFF_RK // 1024)),
            pl.BlockSpec((CHUNK, 2048), lambda b, c: (0, OFF_RV // 2048)),
            pl.BlockSpec((CHUNK, RET_DK // 2), const2),
            pl.BlockSpec((CHUNK, RET_DK // 2), const2),
            pl.BlockSpec((RET_HEADS, CHUNK, CHUNK), const3),
            pl.BlockSpec((RET_HEADS, CHUNK, RET_DV), const3),
            pl.BlockSpec((RET_HEADS, CHUNK, RET_DK), const3),
            pl.BlockSpec((RET_HEADS, 8, RET_DV), const3),
        ],
        out_specs=pl.BlockSpec((1, RET_ROWS, RET_HEADS * RET_DV), lambda b, c: (b, c, 0)),
        out_shape=jax.ShapeDtypeStruct((batch, seq, RET_HEADS * RET_DV), BF16),
        scratch_shapes=[pltpu.VMEM((RET_HEADS, RET_DK, RET_DV), F32)],
        compiler_params=pltpu.CompilerParams(
            dimension_semantics=("parallel", "arbitrary"),
            vmem_limit_bytes=VMEM_LIMIT),
        name="retention",
    )(proj, proj, proj, proj, cbase, sbase, coff, soff, mproj, mproj, mcos, msin, decay, xi, zeta, gchunk)


SB_GROUP = 7


def _softplus2(z):
    return jnp.maximum(z, jnp.log2(1.0 + jnp.exp2(jnp.minimum(z, 126.0))))


def _sb_kernel(q_ref, k_ref, v_ref, mk_ref, mv_ref, tri_ref, o_ref, r_ref):
    nq = q_ref.shape[1] // CHUNK
    lanes = 2 * SB_DH
    q_rows = lambda start, size: q_ref[0, pl.ds(start, size), :]
    k_rows = lambda start, size: k_ref[0, pl.ds(start, size), :]
    v_rows = lambda start, size: v_ref[0, pl.ds(start, size), :]
    meta_k = lambda: mk_ref[...]
    meta_v = lambda: mv_ref[...]
    head0_q = lax.broadcasted_iota(jnp.int32, (CHUNK, lanes), 1) < SB_DH
    head0_kv = lax.broadcasted_iota(jnp.int32, (SB_WIN, lanes), 1) < SB_DH
    t_loc = lax.broadcasted_iota(jnp.int32, (2 * CHUNK, CHUNK), 0) & (CHUNK - 1)
    s_loc = lax.broadcasted_iota(jnp.int32, (2 * CHUNK, CHUNK), 1)
    causal = s_loc < t_loc
    tri = tri_ref[...]

    def stacked_q(row0):
        qf = q_rows(row0, CHUNK).astype(F32) * (SB_DH ** -0.5 * LOG2E)
        both = jnp.concatenate([jnp.where(head0_q, qf, 0.0), jnp.where(head0_q, 0.0, qf)], axis=0)
        return both.astype(BF16)

    mask = lambda x, m: x if m is None else jnp.where(m, x, 0.0)

    def scores(qs, kw):
        return lax.dot_general(qs, kw, (((1,), (1,)), ((), ())), preferred_element_type=F32)

    def log_sticks(z, valid_lo, valid_hi):
        sp = _softplus2(z)
        ln = jnp.concatenate([mask(sp[:, :CHUNK], valid_lo), mask(sp[:, CHUNK:], valid_hi)], axis=1)
        return z - sp, ln[:, 0:1], ln.astype(BF16)

    def later_sums(ln):
        return jnp.dot(ln, tri, preferred_element_type=F32)

    def weights(log_beta, ln0, later, valid_lo, valid_hi, r):
        total = later[:, 0:1] + ln0
        if r is not None:
            later = later + r
            total = total + r
        a = jnp.exp2(log_beta - later)
        a = jnp.concatenate([mask(a[:, :CHUNK], valid_lo), mask(a[:, CHUNK:], valid_hi)], axis=1)
        return a.astype(BF16), total

    def attend(a, vw):
        zero = jnp.zeros_like(vw)
        return (jnp.dot(a[:CHUNK], jnp.where(head0_kv, vw, zero), preferred_element_type=F32)
                + jnp.dot(a[CHUNK:], jnp.where(head0_kv, zero, vw), preferred_element_type=F32))

    def windows(qs, kw, vw, valid_lo, valid_hi, r):
        n = len(qs)
        z, st, later, rs, outs = {}, {}, {}, [None] * n, [None] * n
        for t in range(n + 3):
            if t < n:
                z[t] = scores(qs[t], kw[t])
            if 0 <= t - 1 < n:
                st[t - 1] = log_sticks(z.pop(t - 1), valid_lo, valid_hi)
            if 0 <= t - 2 < n:
                later[t - 2] = later_sums(st[t - 2][2])
            if 0 <= t - 3 < n:
                u = t - 3
                a, rs[u] = weights(st[u][0], st[u][1], later.pop(u), valid_lo, valid_hi, r[u])
                outs[u] = attend(a, vw[u])
        return rs, outs

    def diagonal_windows(blocks, first):
        row0 = [pl.multiple_of(i * CHUNK, CHUNK) for i in blocks]
        qs = [stacked_q(r0) for r0 in row0]
        if first:
            kw = [jnp.concatenate([meta_k(), k_rows(0, CHUNK)], axis=0)]
            vw = [jnp.concatenate([meta_v(), v_rows(0, CHUNK)], axis=0)]
            valid_lo = s_loc >= PAD
        else:
            start = [pl.multiple_of(r0 - CHUNK, CHUNK) for r0 in row0]
            kw = [k_rows(s, SB_WIN) for s in start]
            vw = [v_rows(s, SB_WIN) for s in start]
            valid_lo = None
        r, out = windows(qs, kw, vw, valid_lo, causal, [None] * len(blocks))
        return qs, r, out

    def load_block(rows, meta, j):
        start = pl.multiple_of(jnp.maximum(j, 0) * CHUNK, CHUNK)
        return jnp.where(j < 0, meta(), rows(start, CHUNK))

    def earlier_windows(i, qs, r, acc):
        def window(state):
            j_hi, r, acc = state
            j_lo = j_hi - 1
            kw = jnp.concatenate([load_block(k_rows, meta_k, j_lo), load_block(k_rows, meta_k, j_hi)], axis=0)
            vw = jnp.concatenate([load_block(v_rows, meta_v, j_lo), load_block(v_rows, meta_v, j_hi)], axis=0)
            valid_lo = j_lo * CHUNK + s_loc >= -N_META
            valid_hi = j_hi * CHUNK + s_loc >= -N_META
            r, out = windows([qs], [kw], [vw], valid_lo, valid_hi, [r])
            return j_hi - 2, r[0], acc + out[0]

        def more(state):
            j_hi, r, _ = state
            return (j_hi >= -1) & (jnp.min(r) < SB_STOP)

        return lax.while_loop(more, window, (i - 2, r, acc))[2]

    _, _, out = diagonal_windows([0], True)
    o_ref[0, 0:CHUNK, :] = out[0].astype(BF16)

    def store(i, acc):
        o_ref[0, pl.ds(pl.multiple_of(i * CHUNK, CHUNK), CHUNK), :] = acc.astype(BF16)

    def finish_group(g):
        def block(u, carry):
            r = r_ref[g & 1, u]

            @pl.when(jnp.min(r) < SB_STOP)
            def _():
                i = 1 + g * SB_GROUP + u
                row0 = pl.multiple_of(i * CHUNK, CHUNK)
                acc = o_ref[0, pl.ds(row0, CHUNK), :].astype(F32)
                store(i, earlier_windows(i, stacked_q(row0), r, acc))
            return carry
        lax.fori_loop(0, SB_GROUP, block, 0)

    def group(g, r_prev):
        prev_unfinished = jnp.min(r_prev) < SB_STOP
        blocks = [1 + g * SB_GROUP + u for u in range(SB_GROUP)]
        _, r, out = diagonal_windows(blocks, False)
        for u, i in enumerate(blocks):
            store(i, out[u])
            r_ref[g & 1, u] = r[u]

        @pl.when(prev_unfinished)
        def _():
            finish_group(g - 1)

        return functools.reduce(jnp.minimum, r)

    assert (nq - 1) % SB_GROUP == 0
    n_groups = (nq - 1) // SB_GROUP
    r_last = lax.fori_loop(0, n_groups, group, jnp.full((2 * CHUNK, 1), SB_STOP, F32))

    @pl.when(jnp.min(r_last) < SB_STOP)
    def _():
        finish_group(n_groups - 1)


def _sb_tables():
    j = np.arange(SB_WIN)[:, None]
    s = np.arange(SB_WIN)[None, :]
    return jnp.asarray((j > s).astype(np.float32), BF16)


def _stickbreak(proj, mproj, seq):
    batch = proj.shape[0]
    lanes = 2 * SB_DH
    pairs = SB_HEADS // 2
    return pl.pallas_call(
        _sb_kernel,
        grid=(batch, pairs),
        in_specs=[
            pl.BlockSpec((1, seq, lanes), lambda b, p: (b, 0, OFF_SQ // lanes + p)),
            pl.BlockSpec((1, seq, lanes), lambda b, p: (b, 0, OFF_SK // lanes + p)),
            pl.BlockSpec((1, seq, lanes), lambda b, p: (b, 0, OFF_SV // lanes + p)),
            pl.BlockSpec((CHUNK, lanes), lambda b, p: (0, OFF_SK // lanes + p)),
            pl.BlockSpec((CHUNK, lanes), lambda b, p: (0, OFF_SV // lanes + p)),
            pl.BlockSpec((SB_WIN, SB_WIN), lambda b, p: (0, 0)),
        ],
        out_specs=pl.BlockSpec((1, seq, lanes), lambda b, p: (b, 0, p)),
        out_shape=jax.ShapeDtypeStruct((batch, seq, SB_HEADS * SB_DH), BF16),
        scratch_shapes=[pltpu.VMEM((2, SB_GROUP, 2 * CHUNK, 1), F32)],
        compiler_params=pltpu.CompilerParams(
            dimension_semantics=("parallel", "parallel"),
            vmem_limit_bytes=VMEM_LIMIT),
        name="stickbreak",
    )(proj, proj, proj, mproj, mproj, _sb_tables())


FFN_COLS = 256


def _post_kernel(yr_ref, ys_ref, ga_ref, gb_ref, x_ref, wr_ref, ws_ref, wo_ref, gmix_ref,
                 gpre_ref, wi_ref, wf_ref, gpost_ref, o_ref, u_ref):
    r = jnp.dot(yr_ref[...], wr_ref[...], preferred_element_type=F32)
    s = jnp.dot(ys_ref[...], ws_ref[...], preferred_element_type=F32)
    merged = (jax.nn.sigmoid(ga_ref[...].astype(F32)) * r
              + jax.nn.sigmoid(gb_ref[...].astype(F32)) * s)
    mix = jnp.dot(merged.astype(BF16), wo_ref[...], preferred_element_type=F32)
    h = x_ref[...] + _rms(mix, gmix_ref[...])

    hn = _rms(h, gpre_ref[...]).astype(BF16)
    for c in range(D_FF // FFN_COLS):
        lo = c * FFN_COLS
        a = jnp.dot(hn, wi_ref[:, lo:lo + FFN_COLS], preferred_element_type=F32)
        b = jnp.dot(hn, wi_ref[:, D_FF + lo:D_FF + lo + FFN_COLS], preferred_element_type=F32)
        u_ref[:, lo:lo + FFN_COLS] = (a * jax.nn.sigmoid(a) * b).astype(BF16)
    ff = jnp.dot(u_ref[...], wf_ref[...], preferred_element_type=F32)
    o_ref[...] = h + _rms(ff, gpost_ref[...])


def _post(yret, ysb, proj2d, x2d, wr, ws, wo, gmix, gpre, wi, wf, gpost, tm=512):
    rows = x2d.shape[0]
    row_block = lambda width, col=0: pl.BlockSpec((tm, width), lambda i: (i, col))
    return pl.pallas_call(
        _post_kernel,
        grid=(rows // tm,),
        in_specs=[
            row_block(RET_HEADS * RET_DV),
            row_block(D_MODEL),
            row_block(D_MODEL, OFF_GA // D_MODEL),
            row_block(D_MODEL, OFF_GB // D_MODEL),
            row_block(D_MODEL),
            _resident(wr.shape), _resident(ws.shape), _resident(wo.shape),
            _resident((1, D_MODEL)), _resident((1, D_MODEL)),
            _resident(wi.shape), _resident(wf.shape),
            _resident((1, D_MODEL)),
        ],
        out_specs=row_block(D_MODEL),
        out_shape=jax.ShapeDtypeStruct((rows, D_MODEL), F32),
        scratch_shapes=[pltpu.VMEM((tm, D_FF), BF16)],
        compiler_params=pltpu.CompilerParams(
            dimension_semantics=("parallel",),
            vmem_limit_bytes=VMEM_LIMIT),
        name="post",
    )(yret, ysb, proj2d, proj2d, x2d, wr, ws, wo, gmix, gpre, wi, wf, gpost)


def kernel(x, meta_tokens, w_in, w_ret_out, w_sb_out, w_out, w_ffn_in, w_ffn_out,
           norm_mix_pre, norm_mix_post, norm_ffn_pre, norm_ffn_post):
    batch, seq, d = x.shape
    assert d == D_MODEL and seq % 1024 == 0 and w_in.shape[0] == 1
    x2d = x.reshape(batch * seq, d)
    meta_chunk = jnp.concatenate(
        [jnp.zeros((PAD, d), x.dtype), meta_tokens.astype(x.dtype)], axis=0)

    w_in_b = w_in[0].astype(BF16)
    proj2d, mproj = _in_proj(x2d, meta_chunk, norm_mix_pre, w_in_b, tm=512)
    proj = proj2d.reshape(batch, seq, PROJ_WIDTH)

    yret = _retention(proj, mproj, seq)
    ysb = _stickbreak(proj, mproj, seq)

    bf = lambda w: w[0].astype(BF16)
    out = _post(yret.reshape(batch * seq, -1), ysb.reshape(batch * seq, -1), proj2d, x2d,
                bf(w_ret_out), bf(w_sb_out), bf(w_out), norm_mix_post,
                norm_ffn_pre, bf(w_ffn_in), bf(w_ffn_out), norm_ffn_post)
    return out.reshape(batch, seq, d)
```

```python
import functools

import jax
import jax.numpy as jnp
import numpy as np
from jax import lax
from jax.experimental import pallas as pl
from jax.experimental.pallas import tpu as pltpu

D_MODEL = 1024
N_META = 16
CHUNK = 128
PAD = CHUNK - N_META
RET_HEADS = 4
RET_DK = 256
RET_DV = 512
SB_HEADS = 16
SB_DH = 64
D_FF = 2816
ROPE_BASE = 10000.0
NORM_EPS = 1e-6
GN_EPS = 1e-5

OFF_RQ = 0
OFF_RK = 1024
OFF_RV = 2048
OFF_RG = 4096
OFF_SQ = 6144
OFF_SK = 7168
OFF_SV = 8192
OFF_GA = 9216
OFF_GB = 10240
PROJ_WIDTH = 11264

F32 = jnp.float32
BF16 = jnp.bfloat16

LOG2E = 1.4426950408889634
SB_STOP = 127.0
SB_WIN = 2 * CHUNK

VMEM_LIMIT = 60 * 1024 * 1024


def _rms(xf, g):
    ms = jnp.mean(xf * xf, axis=-1, keepdims=True)
    return xf * lax.rsqrt(ms + NORM_EPS) * g


def _resident(shape):
    return pl.BlockSpec(shape, lambda i: (0,) * len(shape), pipeline_mode=pl.Buffered(1))


FRONT_ROWS = 256
FRONT_COLS = 1024
RET_COLS = OFF_SQ
OUT_COLS = PROJ_WIDTH - RET_COLS


def _rotate(x, cos, sin):
    half = RET_DK // 2
    x1, x2 = x[:, :half], x[:, half:]
    return jnp.concatenate([x1 * cos - x2 * sin, x1 * sin + x2 * cos], axis=1)


def _kv_state(k_rot, v, zeta):
    kz = (k_rot * zeta).astype(BF16)
    return lax.dot_general(kz, v, (((0,), (0,)), ((), ())), preferred_element_type=F32)


def _front_kernel(tiles_per_batch,
                  x_ref, m_ref, g_ref, w_ref, cbase_ref, sbase_ref, coff_ref, soff_ref,
                  mcos_ref, msin_ref, mzeta_ref, decay_ref, xi_ref, zeta_ref, gchunk_ref,
                  o_ref, mo_ref, y_ref, ring_ref, state_ref, mret_ref):
    j = pl.program_id(0)
    cur = j % 2
    prev = 1 - cur
    n_col = PROJ_WIDTH // FRONT_COLS
    n_ret = RET_COLS // FRONT_COLS

    def project(rows, put):
        hn = _rms(rows, g_ref[...]).astype(BF16)
        for t in range(n_col):
            cols = slice(t * FRONT_COLS, (t + 1) * FRONT_COLS)
            put(t, jnp.dot(hn, w_ref[:, cols], preferred_element_type=F32).astype(BF16))
            yield t

    @pl.when(j == 0)
    def _():
        ring_ref[1] = jnp.zeros(ring_ref.shape[1:], BF16)
        state_ref[...] = jnp.zeros(state_ref.shape, F32)

        def put_meta(t, res):
            lo = t * FRONT_COLS
            if OFF_RK <= lo < OFF_RG:
                mret_ref[:, lo - OFF_RK:lo - OFF_RK + FRONT_COLS] = res
            elif lo >= RET_COLS:
                mo_ref[:, lo - RET_COLS:lo - RET_COLS + FRONT_COLS] = res
        for _ in project(m_ref[...], put_meta):
            pass

    @pl.when((j - 1) % tiles_per_batch == 0)
    def _():
        mcos, msin = mcos_ref[...], msin_ref[...]
        for h in range(RET_HEADS):
            k = mret_ref[:, h * RET_DK:(h + 1) * RET_DK].astype(F32)
            v = mret_ref[:, OFF_RV - OFF_RK + h * RET_DV:OFF_RV - OFF_RK + (h + 1) * RET_DV]
            state_ref[h] = _kv_state(_rotate(k, mcos, msin), v, mzeta_ref[h])

    co, so = coff_ref[...], soff_ref[...]
    state = [state_ref[h] for h in range(RET_HEADS)]
    trig = {}
    held = {}

    def phase1(h):
        if not trig:
            cb, sb = cbase_ref[0], sbase_ref[0]
            trig[0] = (cb * co - sb * so, sb * co + cb * so)
        cos, sin = trig[0]
        q = ring_ref[prev, :, OFF_RQ + h * RET_DK:OFF_RQ + (h + 1) * RET_DK].astype(F32)
        k = ring_ref[prev, :, OFF_RK + h * RET_DK:OFF_RK + (h + 1) * RET_DK].astype(F32)
        v = ring_ref[prev, :, OFF_RV + h * RET_DV:OFF_RV + (h + 1) * RET_DV]
        q_rot = (_rotate(q, cos, sin) * (RET_DK ** -0.5)).astype(BF16)
        k_rot = _rotate(k, cos, sin)
        scores = lax.dot_general(q_rot, k_rot.astype(BF16), (((1,), (1,)), ((), ())),
                                 preferred_element_type=F32)
        held[h] = ((scores * decay_ref[h]).astype(BF16), q_rot, (k_rot * zeta_ref[h]).astype(BF16), v)

    def phase2(h):
        sd, q_rot, kz, v = held.pop(h)
        inner = jnp.dot(sd, v, preferred_element_type=F32)
        cross = jnp.dot(q_rot, state[h].astype(BF16), preferred_element_type=F32)
        kv = lax.dot_general(kz, v, (((0,), (0,)), ((), ())), preferred_element_type=F32)
        state[h] = gchunk_ref[h, 0:1, :] * state[h] + kv
        y = inner + cross * xi_ref[h]
        mu = jnp.mean(y, axis=-1, keepdims=True)
        yc = y - mu
        var = jnp.mean(yc * yc, axis=-1, keepdims=True)
        yn = yc * lax.rsqrt(var + GN_EPS)
        gate = ring_ref[prev, :, OFF_RG + h * RET_DV:OFF_RG + (h + 1) * RET_DV].astype(F32)
        y_ref[:, h * RET_DV:(h + 1) * RET_DV] = (gate * jax.nn.sigmoid(gate) * yn).astype(BF16)

    def put(t, res):
        lo = t * FRONT_COLS
        if t < n_ret:
            ring_ref[cur, :, lo:lo + FRONT_COLS] = res
        else:
            o_ref[:, lo - RET_COLS:lo - RET_COLS + FRONT_COLS] = res

    for t in project(x_ref[...], put):
        if t < 2 * RET_HEADS:
            (phase1 if t % 2 == 0 else phase2)(t // 2)
    assert 2 * RET_HEADS <= n_col

    for h in range(RET_HEADS):
        state_ref[h] = state[h]


def _retention_tables(chunk):
    hh = np.arange(RET_HEADS, dtype=np.float64)
    log_g = np.log1p(-(2.0 ** (-5.0 - hh)))
    idx = np.arange(chunk, dtype=np.float64)
    diff = idx[:, None] - idx[None, :]
    decay = np.where(diff >= 0, np.exp(log_g[:, None, None] * np.maximum(diff, 0.0)), 0.0)
    zeta = np.exp(log_g[:, None] * (chunk - 1.0 - idx))[:, :, None]
    xi = np.exp(log_g[:, None] * (idx + 1.0))[:, :, None]
    gchunk = np.exp(log_g * chunk)[:, None, None]
    bc = lambda a, shape: jnp.asarray(np.broadcast_to(a, shape), F32)
    return (bc(decay, (RET_HEADS, chunk, chunk)),
            bc(xi, (RET_HEADS, chunk, RET_DV)),
            bc(zeta, (RET_HEADS, chunk, RET_DK)),
            bc(gchunk, (RET_HEADS, 8, RET_DV)))


def _rope_tables(seq, chunk):
    half = RET_DK // 2
    inv = ROPE_BASE ** (-np.arange(half, dtype=np.float64) / half)
    base = (N_META + chunk * np.arange(seq // chunk, dtype=np.float64))[:, None, None] * inv
    off = np.arange(chunk, dtype=np.float64)[:, None] * inv
    meta = (np.arange(CHUNK, dtype=np.float64) - PAD)[:, None] * inv
    f = lambda a: jnp.asarray(a, F32)
    return (f(np.cos(base)), f(np.sin(base)), f(np.cos(off)), f(np.sin(off)),
            f(np.cos(meta)), f(np.sin(meta)))


def _front(x2d, meta_chunk, gain, w_bf16, seq):
    rows = x2d.shape[0]
    tm = FRONT_ROWS
    n_tiles = rows // tm
    tiles_per_batch = seq // tm
    cbase, sbase, coff, soff, mcos, msin = _rope_tables(seq, tm)
    decay, xi, zeta, gchunk = _retention_tables(tm)
    mzeta = _retention_tables(CHUNK)[2]
    this_tile = lambda j: (jnp.minimum(j, n_tiles - 1), 0)
    last_tile = lambda j: (jnp.maximum(j - 1, 0), 0)
    last_base = lambda j: (jnp.maximum(j - 1, 0) % tiles_per_batch, 0, 0)
    return pl.pallas_call(
        functools.partial(_front_kernel, tiles_per_batch),
        grid=(n_tiles + 1,),
        in_specs=[
            pl.BlockSpec((tm, D_MODEL), this_tile),
            _resident((CHUNK, D_MODEL)),
            _resident((1, D_MODEL)),
            _resident((D_MODEL, PROJ_WIDTH)),
            pl.BlockSpec((1, 1, RET_DK // 2), last_base),
            pl.BlockSpec((1, 1, RET_DK // 2), last_base),
            _resident((tm, RET_DK // 2)), _resident((tm, RET_DK // 2)),
            _resident((CHUNK, RET_DK // 2)), _resident((CHUNK, RET_DK // 2)),
            _resident((RET_HEADS, CHUNK, RET_DK)),
            _resident((RET_HEADS, tm, tm)),
            _resident((RET_HEADS, tm, RET_DV)),
            _resident((RET_HEADS, tm, RET_DK)),
            _resident((RET_HEADS, 8, RET_DV)),
        ],
        out_specs=[
            pl.BlockSpec((tm, OUT_COLS), this_tile),
            pl.BlockSpec((CHUNK, OUT_COLS), lambda j: (0, 0)),
            pl.BlockSpec((tm, RET_HEADS * RET_DV), last_tile),
        ],
        out_shape=[
            jax.ShapeDtypeStruct((rows, OUT_COLS), BF16),
            jax.ShapeDtypeStruct((CHUNK, OUT_COLS), BF16),
            jax.ShapeDtypeStruct((rows, RET_HEADS * RET_DV), BF16),
        ],
        scratch_shapes=[
            pltpu.VMEM((2, tm, RET_COLS), BF16),
            pltpu.VMEM((RET_HEADS, RET_DK, RET_DV), F32),
            pltpu.VMEM((CHUNK, OFF_RG - OFF_RK), BF16),
        ],
        compiler_params=pltpu.CompilerParams(
            dimension_semantics=("arbitrary",),
            vmem_limit_bytes=VMEM_LIMIT),
        name="front",
    )(x2d, meta_chunk, gain, w_bf16, cbase, sbase, coff, soff, mcos, msin, mzeta, decay, xi, zeta, gchunk)


SB_GROUP = 7


def _softplus2(z):
    return jnp.maximum(z, jnp.log2(1.0 + jnp.exp2(jnp.minimum(z, 126.0))))


def _sb_kernel(q_ref, k_ref, v_ref, mk_ref, mv_ref, tri_ref, o_ref, r_ref):
    nq = q_ref.shape[1] // CHUNK
    lanes = 2 * SB_DH
    q_rows = lambda start, size: q_ref[0, pl.ds(start, size), :]
    k_rows = lambda start, size: k_ref[0, pl.ds(start, size), :]
    v_rows = lambda start, size: v_ref[0, pl.ds(start, size), :]
    meta_k = lambda: mk_ref[...]
    meta_v = lambda: mv_ref[...]
    head0_q = lax.broadcasted_iota(jnp.int32, (CHUNK, lanes), 1) < SB_DH
    head0_kv = lax.broadcasted_iota(jnp.int32, (SB_WIN, lanes), 1) < SB_DH
    t_loc = lax.broadcasted_iota(jnp.int32, (2 * CHUNK, CHUNK), 0) & (CHUNK - 1)
    s_loc = lax.broadcasted_iota(jnp.int32, (2 * CHUNK, CHUNK), 1)
    causal = s_loc < t_loc
    tri = tri_ref[...]

    def stacked_q(row0):
        qf = q_rows(row0, CHUNK).astype(F32) * (SB_DH ** -0.5 * LOG2E)
        both = jnp.concatenate([jnp.where(head0_q, qf, 0.0), jnp.where(head0_q, 0.0, qf)], axis=0)
        return both.astype(BF16)

    mask = lambda x, m: x if m is None else jnp.where(m, x, 0.0)

    def scores(qs, kw):
        return lax.dot_general(qs, kw, (((1,), (1,)), ((), ())), preferred_element_type=F32)

    def log_sticks(z, valid_lo, valid_hi):
        sp = _softplus2(z)
        ln = jnp.concatenate([mask(sp[:, :CHUNK], valid_lo), mask(sp[:, CHUNK:], valid_hi)], axis=1)
        return z - sp, ln[:, 0:1], ln.astype(BF16)

    def later_sums(ln):
        return jnp.dot(ln, tri, preferred_element_type=F32)

    def weights(log_beta, ln0, later, valid_lo, valid_hi, r):
        total = later[:, 0:1] + ln0
        if r is not None:
            later = later + r
            total = total + r
        a = jnp.exp2(log_beta - later)
        a = jnp.concatenate([mask(a[:, :CHUNK], valid_lo), mask(a[:, CHUNK:], valid_hi)], axis=1)
        return a.astype(BF16), total

    def attend(a, vw):
        zero = jnp.zeros_like(vw)
        return (jnp.dot(a[:CHUNK], jnp.where(head0_kv, vw, zero), preferred_element_type=F32)
                + jnp.dot(a[CHUNK:], jnp.where(head0_kv, zero, vw), preferred_element_type=F32))

    def windows(qs, kw, vw, valid_lo, valid_hi, r):
        n = len(qs)
        z, st, later, rs, outs = {}, {}, {}, [None] * n, [None] * n
        for t in range(n + 3):
            if t < n:
                z[t] = scores(qs[t], kw[t])
            if 0 <= t - 1 < n:
                st[t - 1] = log_sticks(z.pop(t - 1), valid_lo, valid_hi)
            if 0 <= t - 2 < n:
                later[t - 2] = later_sums(st[t - 2][2])
            if 0 <= t - 3 < n:
                u = t - 3
                a, rs[u] = weights(st[u][0], st[u][1], later.pop(u), valid_lo, valid_hi, r[u])
                outs[u] = attend(a, vw[u])
        return rs, outs

    def diagonal_windows(blocks, first):
        row0 = [pl.multiple_of(i * CHUNK, CHUNK) for i in blocks]
        qs = [stacked_q(r0) for r0 in row0]
        if first:
            kw = [jnp.concatenate([meta_k(), k_rows(0, CHUNK)], axis=0)]
            vw = [jnp.concatenate([meta_v(), v_rows(0, CHUNK)], axis=0)]
            valid_lo = s_loc >= PAD
        else:
            start = [pl.multiple_of(r0 - CHUNK, CHUNK) for r0 in row0]
            kw = [k_rows(s, SB_WIN) for s in start]
            vw = [v_rows(s, SB_WIN) for s in start]
            valid_lo = None
        r, out = windows(qs, kw, vw, valid_lo, causal, [None] * len(blocks))
        return qs, r, out

    def load_block(rows, meta, j):
        start = pl.multiple_of(jnp.maximum(j, 0) * CHUNK, CHUNK)
        return jnp.where(j < 0, meta(), rows(start, CHUNK))

    def earlier_windows(i, qs, r, acc):
        def window(state):
            j_hi, r, acc = state
            j_lo = j_hi - 1
            kw = jnp.concatenate([load_block(k_rows, meta_k, j_lo), load_block(k_rows, meta_k, j_hi)], axis=0)
            vw = jnp.concatenate([load_block(v_rows, meta_v, j_lo), load_block(v_rows, meta_v, j_hi)], axis=0)
            valid_lo = j_lo * CHUNK + s_loc >= -N_META
            valid_hi = j_hi * CHUNK + s_loc >= -N_META
            r, out = windows([qs], [kw], [vw], valid_lo, valid_hi, [r])
            return j_hi - 2, r[0], acc + out[0]

        def more(state):
            j_hi, r, _ = state
            return (j_hi >= -1) & (jnp.min(r) < SB_STOP)

        return lax.while_loop(more, window, (i - 2, r, acc))[2]

    _, _, out = diagonal_windows([0], True)
    o_ref[0, 0:CHUNK, :] = out[0].astype(BF16)

    def store(i, acc):
        o_ref[0, pl.ds(pl.multiple_of(i * CHUNK, CHUNK), CHUNK), :] = acc.astype(BF16)

    def finish_group(g):
        def block(u, carry):
            r = r_ref[g & 1, u]

            @pl.when(jnp.min(r) < SB_STOP)
            def _():
                i = 1 + g * SB_GROUP + u
                row0 = pl.multiple_of(i * CHUNK, CHUNK)
                acc = o_ref[0, pl.ds(row0, CHUNK), :].astype(F32)
                store(i, earlier_windows(i, stacked_q(row0), r, acc))
            return carry
        lax.fori_loop(0, SB_GROUP, block, 0)

    def group(g, r_prev):
        prev_unfinished = jnp.min(r_prev) < SB_STOP
        blocks = [1 + g * SB_GROUP + u for u in range(SB_GROUP)]
        _, r, out = diagonal_windows(blocks, False)
        for u, i in enumerate(blocks):
            store(i, out[u])
            r_ref[g & 1, u] = r[u]

        @pl.when(prev_unfinished)
        def _():
            finish_group(g - 1)

        return functools.reduce(jnp.minimum, r)

    assert (nq - 1) % SB_GROUP == 0
    n_groups = (nq - 1) // SB_GROUP
    r_last = lax.fori_loop(0, n_groups, group, jnp.full((2 * CHUNK, 1), SB_STOP, F32))

    @pl.when(jnp.min(r_last) < SB_STOP)
    def _():
        finish_group(n_groups - 1)


def _sb_tables():
    j = np.arange(SB_WIN)[:, None]
    s = np.arange(SB_WIN)[None, :]
    return jnp.asarray((j > s).astype(np.float32), BF16)


def _stickbreak(proj, mproj, seq):
    batch = proj.shape[0]
    lanes = 2 * SB_DH
    pairs = SB_HEADS // 2
    return pl.pallas_call(
        _sb_kernel,
        grid=(batch, pairs),
        in_specs=[
            pl.BlockSpec((1, seq, lanes), lambda b, p: (b, 0, (OFF_SQ - RET_COLS) // lanes + p)),
            pl.BlockSpec((1, seq, lanes), lambda b, p: (b, 0, (OFF_SK - RET_COLS) // lanes + p)),
            pl.BlockSpec((1, seq, lanes), lambda b, p: (b, 0, (OFF_SV - RET_COLS) // lanes + p)),
            pl.BlockSpec((CHUNK, lanes), lambda b, p: (0, (OFF_SK - RET_COLS) // lanes + p)),
            pl.BlockSpec((CHUNK, lanes), lambda b, p: (0, (OFF_SV - RET_COLS) // lanes + p)),
            pl.BlockSpec((SB_WIN, SB_WIN), lambda b, p: (0, 0)),
        ],
        out_specs=pl.BlockSpec((1, seq, lanes), lambda b, p: (b, 0, p)),
        out_shape=jax.ShapeDtypeStruct((batch, seq, SB_HEADS * SB_DH), BF16),
        scratch_shapes=[pltpu.VMEM((2, SB_GROUP, 2 * CHUNK, 1), F32)],
        compiler_params=pltpu.CompilerParams(
            dimension_semantics=("parallel", "parallel"),
            vmem_limit_bytes=VMEM_LIMIT),
        name="stickbreak",
    )(proj, proj, proj, mproj, mproj, _sb_tables())


FFN_COLS = 256
POST_PARTS = 2


def _post_kernel(yr_ref, ys_ref, ga_ref, gb_ref, x_ref, wr_ref, ws_ref, wo_ref, gmix_ref,
                 gpre_ref, wi_ref, wf_ref, gpost_ref, o_ref, u_ref):
    tm = o_ref.shape[0]
    parts = [slice(p * tm // POST_PARTS, (p + 1) * tm // POST_PARTS) for p in range(POST_PARTS)]
    dot = functools.partial(jnp.dot, preferred_element_type=F32)

    r = [dot(yr_ref[p, :], wr_ref[...]) for p in parts]
    s = [dot(ys_ref[p, :], ws_ref[...]) for p in parts]
    mix = []
    for i, p in enumerate(parts):
        merged = (jax.nn.sigmoid(ga_ref[p, :].astype(F32)) * r[i]
                  + jax.nn.sigmoid(gb_ref[p, :].astype(F32)) * s[i])
        mix.append(dot(merged.astype(BF16), wo_ref[...]))
    h, hn = [], []
    for i, p in enumerate(parts):
        h.append(x_ref[p, :] + _rms(mix[i], gmix_ref[...]))
        hn.append(_rms(h[i], gpre_ref[...]).astype(BF16))
    for c in range(D_FF // FFN_COLS):
        lo = c * FFN_COLS
        for i, p in enumerate(parts):
            a = dot(hn[i], wi_ref[:, lo:lo + FFN_COLS])
            b = dot(hn[i], wi_ref[:, D_FF + lo:D_FF + lo + FFN_COLS])
            u_ref[p, lo:lo + FFN_COLS] = (a * jax.nn.sigmoid(a) * b).astype(BF16)
    ff = [dot(u_ref[p, :], wf_ref[...]) for p in parts]
    for i, p in enumerate(parts):
        o_ref[p, :] = h[i] + _rms(ff[i], gpost_ref[...])


def _post(yret, ysb, proj2d, x2d, wr, ws, wo, gmix, gpre, wi, wf, gpost, tm=512):
    rows = x2d.shape[0]
    row_block = lambda width, col=0: pl.BlockSpec((tm, width), lambda i: (i, col))
    return pl.pallas_call(
        _post_kernel,
        grid=(rows // tm,),
        in_specs=[
            row_block(RET_HEADS * RET_DV),
            row_block(D_MODEL),
            row_block(D_MODEL, (OFF_GA - RET_COLS) // D_MODEL),
            row_block(D_MODEL, (OFF_GB - RET_COLS) // D_MODEL),
            row_block(D_MODEL),
            _resident(wr.shape), _resident(ws.shape), _resident(wo.shape),
            _resident((1, D_MODEL)), _resident((1, D_MODEL)),
            _resident(wi.shape), _resident(wf.shape),
            _resident((1, D_MODEL)),
        ],
        out_specs=row_block(D_MODEL),
        out_shape=jax.ShapeDtypeStruct((rows, D_MODEL), F32),
        scratch_shapes=[pltpu.VMEM((tm, D_FF), BF16)],
        compiler_params=pltpu.CompilerParams(
            dimension_semantics=("parallel",),
            vmem_limit_bytes=VMEM_LIMIT),
        name="post",
    )(yret, ysb, proj2d, proj2d, x2d, wr, ws, wo, gmix, gpre, wi, wf, gpost)


def kernel(x, meta_tokens, w_in, w_ret_out, w_sb_out, w_out, w_ffn_in, w_ffn_out,
           norm_mix_pre, norm_mix_post, norm_ffn_pre, norm_ffn_post):
    batch, seq, d = x.shape
    assert d == D_MODEL and seq % 1024 == 0 and w_in.shape[0] == 1
    x2d = x.reshape(batch * seq, d)
    meta_chunk = jnp.concatenate(
        [jnp.zeros((PAD, d), x.dtype), meta_tokens.astype(x.dtype)], axis=0)

    proj2d, mproj, yret = _front(x2d, meta_chunk, norm_mix_pre, w_in[0].astype(BF16), seq)
    ysb = _stickbreak(proj2d.reshape(batch, seq, OUT_COLS), mproj, seq)

    bf = lambda w: w[0].astype(BF16)
    out = _post(yret, ysb.reshape(batch * seq, -1), proj2d, x2d,
                bf(w_ret_out), bf(w_sb_out), bf(w_out), norm_mix_post,
                norm_ffn_pre, bf(w_ffn_in), bf(w_ffn_out), norm_ffn_post)
    return out.reshape(batch, seq, d)
```

```python
import functools

import jax
import jax.numpy as jnp
import numpy as np
from jax import lax
from jax.experimental import pallas as pl
from jax.experimental.pallas import tpu as pltpu

D_MODEL = 1024
N_META = 16
CHUNK = 128
PAD = CHUNK - N_META
RET_HEADS = 4
RET_DK = 256
RET_DV = 512
SB_HEADS = 16
SB_DH = 64
D_FF = 2816
ROPE_BASE = 10000.0
NORM_EPS = 1e-6
GN_EPS = 1e-5

OFF_RQ = 0
OFF_RK = 1024
OFF_RV = 2048
OFF_RG = 4096
OFF_SQ = 6144
OFF_SK = 7168
OFF_SV = 8192
OFF_GA = 9216
OFF_GB = 10240
PROJ_WIDTH = 11264

F32 = jnp.float32
BF16 = jnp.bfloat16

LOG2E = 1.4426950408889634
SB_STOP = 127.0
SB_WIN = 2 * CHUNK

VMEM_LIMIT = 60 * 1024 * 1024


def _rms(xf, g):
    ms = jnp.mean(xf * xf, axis=-1, keepdims=True)
    return xf * lax.rsqrt(ms + NORM_EPS) * g


def _resident(shape):
    return pl.BlockSpec(shape, lambda i: (0,) * len(shape), pipeline_mode=pl.Buffered(1))


FRONT_ROWS = 256
FRONT_COLS = 1024
RET_COLS = OFF_SQ
OUT_COLS = PROJ_WIDTH - RET_COLS


def _rotate(x, cos, sin):
    half = RET_DK // 2
    x1, x2 = x[:, :half], x[:, half:]
    return jnp.concatenate([x1 * cos - x2 * sin, x1 * sin + x2 * cos], axis=1)


def _kv_state(k_rot, v, zeta):
    kz = (k_rot * zeta).astype(BF16)
    return lax.dot_general(kz, v, (((0,), (0,)), ((), ())), preferred_element_type=F32)


def _front_kernel(tiles_per_batch, n_cast,
                  x_ref, m_ref, g_ref, w_ref, cbase_ref, sbase_ref, coff_ref, soff_ref,
                  mcos_ref, msin_ref, mzeta_ref, decay_ref, xi_ref, zeta_ref, gchunk_ref, *rest):
    cast_in, (o_ref, mo_ref, y_ref), rest = rest[:n_cast], rest[n_cast:n_cast + 3], rest[n_cast + 3:]
    cast_out, (ring_ref, state_ref, mret_ref) = rest[:n_cast], rest[n_cast:]
    j = pl.program_id(0)

    for src, dst in zip(cast_in, cast_out):
        dst[...] = src[...].astype(BF16)
    cur = j % 2
    prev = 1 - cur
    n_col = PROJ_WIDTH // FRONT_COLS
    n_ret = RET_COLS // FRONT_COLS

    def normed(rows):
        return _rms(rows, g_ref[...]).astype(BF16)

    def project(hn, put):
        for t in range(n_col):
            cols = slice(t * FRONT_COLS, (t + 1) * FRONT_COLS)
            put(t, jnp.dot(hn, w_ref[:, cols], preferred_element_type=F32).astype(BF16))
            yield t

    @pl.when(j == 0)
    def _():
        ring_ref[1] = jnp.zeros(ring_ref.shape[1:], BF16)
        state_ref[...] = jnp.zeros(state_ref.shape, F32)

        def put_meta(t, res):
            lo = t * FRONT_COLS
            if OFF_RK <= lo < OFF_RG:
                mret_ref[:, lo - OFF_RK:lo - OFF_RK + FRONT_COLS] = res
            elif lo >= RET_COLS:
                mo_ref[:, lo - RET_COLS:lo - RET_COLS + FRONT_COLS] = res
        for _ in project(normed(m_ref[...]), put_meta):
            pass

    @pl.when((j - 1) % tiles_per_batch == 0)
    def _():
        mcos, msin = mcos_ref[...], msin_ref[...]
        for h in range(RET_HEADS):
            k = mret_ref[:, h * RET_DK:(h + 1) * RET_DK].astype(F32)
            v = mret_ref[:, OFF_RV - OFF_RK + h * RET_DV:OFF_RV - OFF_RK + (h + 1) * RET_DV]
            state_ref[h] = _kv_state(_rotate(k, mcos, msin), v, mzeta_ref[h])

    co, so = coff_ref[...], soff_ref[...]
    state = [state_ref[h] for h in range(RET_HEADS)]
    trig = {}
    held = {}

    def phase1(h):
        if not trig:
            cb, sb = cbase_ref[0], sbase_ref[0]
            trig[0] = (cb * co - sb * so, sb * co + cb * so)
        cos, sin = trig[0]
        q = ring_ref[prev, :, OFF_RQ + h * RET_DK:OFF_RQ + (h + 1) * RET_DK].astype(F32)
        k = ring_ref[prev, :, OFF_RK + h * RET_DK:OFF_RK + (h + 1) * RET_DK].astype(F32)
        v = ring_ref[prev, :, OFF_RV + h * RET_DV:OFF_RV + (h + 1) * RET_DV]
        q_rot = (_rotate(q, cos, sin) * (RET_DK ** -0.5)).astype(BF16)
        k_rot = _rotate(k, cos, sin)
        scores = lax.dot_general(q_rot, k_rot.astype(BF16), (((1,), (1,)), ((), ())),
                                 preferred_element_type=F32)
        held[h] = ((scores * decay_ref[h]).astype(BF16), q_rot, (k_rot * zeta_ref[h]).astype(BF16), v)

    def phase2(h):
        sd, q_rot, kz, v = held.pop(h)
        inner = jnp.dot(sd, v, preferred_element_type=F32)
        cross = jnp.dot(q_rot, state[h].astype(BF16), preferred_element_type=F32)
        kv = lax.dot_general(kz, v, (((0,), (0,)), ((), ())), preferred_element_type=F32)
        state[h] = gchunk_ref[h, 0:1, :] * state[h] + kv
        y = inner + cross * xi_ref[h]
        mu = jnp.mean(y, axis=-1, keepdims=True)
        yc = y - mu
        var = jnp.mean(yc * yc, axis=-1, keepdims=True)
        yn = yc * lax.rsqrt(var + GN_EPS)
        gate = ring_ref[prev, :, OFF_RG + h * RET_DV:OFF_RG + (h + 1) * RET_DV].astype(F32)
        y_ref[:, h * RET_DV:(h + 1) * RET_DV] = (gate * jax.nn.sigmoid(gate) * yn).astype(BF16)

    def put(t, res):
        lo = t * FRONT_COLS
        if t < n_ret:
            ring_ref[cur, :, lo:lo + FRONT_COLS] = res
        else:
            o_ref[:, lo - RET_COLS:lo - RET_COLS + FRONT_COLS] = res

    for t in project(normed(x_ref[...]), put):
        if t < 2 * RET_HEADS:
            (phase1 if t % 2 == 0 else phase2)(t // 2)
    assert 2 * RET_HEADS <= n_col

    for h in range(RET_HEADS):
        state_ref[h] = state[h]


def _retention_tables(chunk):
    hh = np.arange(RET_HEADS, dtype=np.float64)
    log_g = np.log1p(-(2.0 ** (-5.0 - hh)))
    idx = np.arange(chunk, dtype=np.float64)
    diff = idx[:, None] - idx[None, :]
    decay = np.where(diff >= 0, np.exp(log_g[:, None, None] * np.maximum(diff, 0.0)), 0.0)
    zeta = np.exp(log_g[:, None] * (chunk - 1.0 - idx))[:, :, None]
    xi = np.exp(log_g[:, None] * (idx + 1.0))[:, :, None]
    gchunk = np.exp(log_g * chunk)[:, None, None]
    bc = lambda a, shape: jnp.asarray(np.broadcast_to(a, shape), F32)
    return (bc(decay, (RET_HEADS, chunk, chunk)),
            bc(xi, (RET_HEADS, chunk, RET_DV)),
            bc(zeta, (RET_HEADS, chunk, RET_DK)),
            bc(gchunk, (RET_HEADS, 8, RET_DV)))


def _rope_tables(seq, chunk):
    half = RET_DK // 2
    inv = ROPE_BASE ** (-np.arange(half, dtype=np.float64) / half)
    base = (N_META + chunk * np.arange(seq // chunk, dtype=np.float64))[:, None, None] * inv
    off = np.arange(chunk, dtype=np.float64)[:, None] * inv
    meta = (np.arange(CHUNK, dtype=np.float64) - PAD)[:, None] * inv
    f = lambda a: jnp.asarray(a, F32)
    return (f(np.cos(base)), f(np.sin(base)), f(np.cos(off)), f(np.sin(off)),
            f(np.cos(meta)), f(np.sin(meta)))


def _front(x2d, meta_chunk, gain, w_bf16, seq, to_cast):
    rows = x2d.shape[0]
    tm = FRONT_ROWS
    n_tiles = rows // tm
    tiles_per_batch = seq // tm
    cbase, sbase, coff, soff, mcos, msin = _rope_tables(seq, tm)
    decay, xi, zeta, gchunk = _retention_tables(tm)
    mzeta = _retention_tables(CHUNK)[2]
    this_tile = lambda j: (jnp.minimum(j, n_tiles - 1), 0)
    last_tile = lambda j: (jnp.maximum(j - 1, 0), 0)
    last_base = lambda j: (jnp.maximum(j - 1, 0) % tiles_per_batch, 0, 0)
    slab = D_MODEL // n_tiles
    assert slab * n_tiles == D_MODEL and slab % 16 == 0
    cast_specs = [pl.BlockSpec((slab, w.shape[1]), this_tile) for w in to_cast]
    return pl.pallas_call(
        functools.partial(_front_kernel, tiles_per_batch, len(to_cast)),
        grid=(n_tiles + 1,),
        in_specs=[
            pl.BlockSpec((tm, D_MODEL), this_tile),
            _resident((CHUNK, D_MODEL)),
            _resident((1, D_MODEL)),
            _resident((D_MODEL, PROJ_WIDTH)),
            pl.BlockSpec((1, 1, RET_DK // 2), last_base),
            pl.BlockSpec((1, 1, RET_DK // 2), last_base),
            _resident((tm, RET_DK // 2)), _resident((tm, RET_DK // 2)),
            _resident((CHUNK, RET_DK // 2)), _resident((CHUNK, RET_DK // 2)),
            _resident((RET_HEADS, CHUNK, RET_DK)),
            _resident((RET_HEADS, tm, tm)),
            _resident((RET_HEADS, tm, RET_DV)),
            _resident((RET_HEADS, tm, RET_DK)),
            _resident((RET_HEADS, 8, RET_DV)),
        ] + cast_specs,
        out_specs=[
            pl.BlockSpec((tm, OUT_COLS), this_tile),
            pl.BlockSpec((CHUNK, OUT_COLS), lambda j: (0, 0)),
            pl.BlockSpec((tm, RET_HEADS * RET_DV), last_tile),
        ] + cast_specs,
        out_shape=[
            jax.ShapeDtypeStruct((rows, OUT_COLS), BF16),
            jax.ShapeDtypeStruct((CHUNK, OUT_COLS), BF16),
            jax.ShapeDtypeStruct((rows, RET_HEADS * RET_DV), BF16),
        ] + [jax.ShapeDtypeStruct(w.shape, BF16) for w in to_cast],
        scratch_shapes=[
            pltpu.VMEM((2, tm, RET_COLS), BF16),
            pltpu.VMEM((RET_HEADS, RET_DK, RET_DV), F32),
            pltpu.VMEM((CHUNK, OFF_RG - OFF_RK), BF16),
        ],
        compiler_params=pltpu.CompilerParams(
            dimension_semantics=("arbitrary",),
            vmem_limit_bytes=VMEM_LIMIT),
        name="front",
    )(x2d, meta_chunk, gain, w_bf16, cbase, sbase, coff, soff, mcos, msin, mzeta, decay, xi, zeta, gchunk,
      *to_cast)


SB_GROUP = 9


def _softplus2(z):
    return jnp.maximum(z, jnp.log2(1.0 + jnp.exp2(jnp.minimum(z, 126.0))))


def _sb_kernel(q_ref, k_ref, v_ref, mk_ref, mv_ref, tri_ref, o_ref, r_ref):
    nq = q_ref.shape[1] // CHUNK
    lanes = 2 * SB_DH
    q_rows = lambda start, size: q_ref[0, pl.ds(start, size), :]
    k_rows = lambda start, size: k_ref[0, pl.ds(start, size), :]
    v_rows = lambda start, size: v_ref[0, pl.ds(start, size), :]
    meta_k = lambda: mk_ref[...]
    meta_v = lambda: mv_ref[...]
    head0_q = lax.broadcasted_iota(jnp.int32, (CHUNK, lanes), 1) < SB_DH
    head0_kv = lax.broadcasted_iota(jnp.int32, (SB_WIN, lanes), 1) < SB_DH
    t_loc = lax.broadcasted_iota(jnp.int32, (2 * CHUNK, CHUNK), 0) & (CHUNK - 1)
    s_loc = lax.broadcasted_iota(jnp.int32, (2 * CHUNK, CHUNK), 1)
    causal = s_loc < t_loc
    tri = tri_ref[...]

    def stacked_q(row0):
        qf = q_rows(row0, CHUNK).astype(F32) * (SB_DH ** -0.5 * LOG2E)
        both = jnp.concatenate([jnp.where(head0_q, qf, 0.0), jnp.where(head0_q, 0.0, qf)], axis=0)
        return both.astype(BF16)

    mask = lambda x, m: x if m is None else jnp.where(m, x, 0.0)

    def scores(qs, kw):
        return lax.dot_general(qs, kw, (((1,), (1,)), ((), ())), preferred_element_type=F32)

    def log_sticks(z, valid_lo, valid_hi):
        sp = _softplus2(z)
        ln = jnp.concatenate([mask(sp[:, :CHUNK], valid_lo), mask(sp[:, CHUNK:], valid_hi)], axis=1)
        return z - sp, ln[:, 0:1], ln.astype(BF16)

    def later_sums(ln):
        return jnp.dot(ln, tri, preferred_element_type=F32)

    def weights(log_beta, ln0, later, valid_lo, valid_hi, r):
        total = later[:, 0:1] + ln0
        if r is not None:
            later = later + r
            total = total + r
        a = jnp.exp2(log_beta - later)
        a = jnp.concatenate([mask(a[:, :CHUNK], valid_lo), mask(a[:, CHUNK:], valid_hi)], axis=1)
        return a.astype(BF16), total

    def attend(a, vw):
        zero = jnp.zeros_like(vw)
        return (jnp.dot(a[:CHUNK], jnp.where(head0_kv, vw, zero), preferred_element_type=F32)
                + jnp.dot(a[CHUNK:], jnp.where(head0_kv, zero, vw), preferred_element_type=F32))

    def windows(qs, kw, vw, valid_lo, valid_hi, r):
        n = len(qs)
        z, st, later, rs, outs = {}, {}, {}, [None] * n, [None] * n
        for t in range(n + 3):
            if t < n:
                z[t] = scores(qs[t], kw[t])
            if 0 <= t - 1 < n:
                st[t - 1] = log_sticks(z.pop(t - 1), valid_lo, valid_hi)
            if 0 <= t - 2 < n:
                later[t - 2] = later_sums(st[t - 2][2])
            if 0 <= t - 3 < n:
                u = t - 3
                a, rs[u] = weights(st[u][0], st[u][1], later.pop(u), valid_lo, valid_hi, r[u])
                outs[u] = attend(a, vw[u])
        return rs, outs

    def diagonal_windows(blocks, first):
        row0 = [pl.multiple_of(i * CHUNK, CHUNK) for i in blocks]
        qs = [stacked_q(r0) for r0 in row0]
        if first:
            kw = [jnp.concatenate([meta_k(), k_rows(0, CHUNK)], axis=0)]
            vw = [jnp.concatenate([meta_v(), v_rows(0, CHUNK)], axis=0)]
            valid_lo = s_loc >= PAD
        else:
            start = [pl.multiple_of(r0 - CHUNK, CHUNK) for r0 in row0]
            kw = [k_rows(s, SB_WIN) for s in start]
            vw = [v_rows(s, SB_WIN) for s in start]
            valid_lo = None
        r, out = windows(qs, kw, vw, valid_lo, causal, [None] * len(blocks))
        return qs, r, out

    def load_block(rows, meta, j):
        start = pl.multiple_of(jnp.maximum(j, 0) * CHUNK, CHUNK)
        return jnp.where(j < 0, meta(), rows(start, CHUNK))

    def earlier_windows(i, qs, r, acc):
        def window(state):
            j_hi, r, acc = state
            j_lo = j_hi - 1
            kw = jnp.concatenate([load_block(k_rows, meta_k, j_lo), load_block(k_rows, meta_k, j_hi)], axis=0)
            vw = jnp.concatenate([load_block(v_rows, meta_v, j_lo), load_block(v_rows, meta_v, j_hi)], axis=0)
            valid_lo = j_lo * CHUNK + s_loc >= -N_META
            valid_hi = j_hi * CHUNK + s_loc >= -N_META
            r, out = windows([qs], [kw], [vw], valid_lo, valid_hi, [r])
            return j_hi - 2, r[0], acc + out[0]

        def more(state):
            j_hi, r, _ = state
            return (j_hi >= -1) & (jnp.min(r) < SB_STOP)

        return lax.while_loop(more, window, (i - 2, r, acc))[2]

    _, _, out = diagonal_windows([0], True)
    o_ref[0, 0:CHUNK, :] = out[0].astype(BF16)

    def store(i, acc):
        o_ref[0, pl.ds(pl.multiple_of(i * CHUNK, CHUNK), CHUNK), :] = acc.astype(BF16)

    def finish_group(g):
        def block(u, carry):
            r = r_ref[g & 1, u]

            @pl.when(jnp.min(r) < SB_STOP)
            def _():
                i = 1 + g * SB_GROUP + u
                row0 = pl.multiple_of(i * CHUNK, CHUNK)
                acc = o_ref[0, pl.ds(row0, CHUNK), :].astype(F32)
                store(i, earlier_windows(i, stacked_q(row0), r, acc))
            return carry
        lax.fori_loop(0, SB_GROUP, block, 0)

    def group(g, r_prev):
        prev_unfinished = jnp.min(r_prev) < SB_STOP
        blocks = [1 + g * SB_GROUP + u for u in range(SB_GROUP)]
        _, r, out = diagonal_windows(blocks, False)
        for u, i in enumerate(blocks):
            store(i, out[u])
            r_ref[g & 1, u] = r[u]

        @pl.when(prev_unfinished)
        def _():
            finish_group(g - 1)

        return functools.reduce(jnp.minimum, r)

    assert (nq - 1) % SB_GROUP == 0
    n_groups = (nq - 1) // SB_GROUP
    r_last = lax.fori_loop(0, n_groups, group, jnp.full((2 * CHUNK, 1), SB_STOP, F32))

    @pl.when(jnp.min(r_last) < SB_STOP)
    def _():
        finish_group(n_groups - 1)


def _sb_tables():
    j = np.arange(SB_WIN)[:, None]
    s = np.arange(SB_WIN)[None, :]
    return jnp.asarray((j > s).astype(np.float32), BF16)


def _stickbreak(proj, mproj, seq):
    batch = proj.shape[0]
    lanes = 2 * SB_DH
    pairs = SB_HEADS // 2
    return pl.pallas_call(
        _sb_kernel,
        grid=(batch, pairs),
        in_specs=[
            pl.BlockSpec((1, seq, lanes), lambda b, p: (b, 0, (OFF_SQ - RET_COLS) // lanes + p)),
            pl.BlockSpec((1, seq, lanes), lambda b, p: (b, 0, (OFF_SK - RET_COLS) // lanes + p)),
            pl.BlockSpec((1, seq, lanes), lambda b, p: (b, 0, (OFF_SV - RET_COLS) // lanes + p)),
            pl.BlockSpec((CHUNK, lanes), lambda b, p: (0, (OFF_SK - RET_COLS) // lanes + p)),
            pl.BlockSpec((CHUNK, lanes), lambda b, p: (0, (OFF_SV - RET_COLS) // lanes + p)),
            pl.BlockSpec((SB_WIN, SB_WIN), lambda b, p: (0, 0)),
        ],
        out_specs=pl.BlockSpec((1, seq, lanes), lambda b, p: (b, 0, p)),
        out_shape=jax.ShapeDtypeStruct((batch, seq, SB_HEADS * SB_DH), BF16),
        scratch_shapes=[pltpu.VMEM((2, SB_GROUP, 2 * CHUNK, 1), F32)],
        compiler_params=pltpu.CompilerParams(
            dimension_semantics=("parallel", "parallel"),
            vmem_limit_bytes=VMEM_LIMIT),
        name="stickbreak",
    )(proj, proj, proj, mproj, mproj, _sb_tables())


FFN_COLS = 256
POST_PARTS = 2


def _post_kernel(yr_ref, ys_ref, ga_ref, gb_ref, x_ref, wr_ref, ws_ref, wo_ref, gmix_ref,
                 gpre_ref, wi_ref, wf_ref, gpost_ref, o_ref, u_ref):
    tm = o_ref.shape[0]
    parts = [slice(p * tm // POST_PARTS, (p + 1) * tm // POST_PARTS) for p in range(POST_PARTS)]
    dot = functools.partial(jnp.dot, preferred_element_type=F32)

    r = [dot(yr_ref[p, :], wr_ref[...]) for p in parts]
    s = [dot(ys_ref[p, :], ws_ref[...]) for p in parts]
    mix = []
    for i, p in enumerate(parts):
        merged = (jax.nn.sigmoid(ga_ref[p, :].astype(F32)) * r[i]
                  + jax.nn.sigmoid(gb_ref[p, :].astype(F32)) * s[i])
        mix.append(dot(merged.astype(BF16), wo_ref[...]))
    h, hn = [], []
    for i, p in enumerate(parts):
        h.append(x_ref[p, :] + _rms(mix[i], gmix_ref[...]))
        hn.append(_rms(h[i], gpre_ref[...]).astype(BF16))
    for c in range(D_FF // FFN_COLS):
        lo = c * FFN_COLS
        for i, p in enumerate(parts):
            a = dot(hn[i], wi_ref[:, lo:lo + FFN_COLS])
            b = dot(hn[i], wi_ref[:, D_FF + lo:D_FF + lo + FFN_COLS])
            u_ref[p, lo:lo + FFN_COLS] = (a * jax.nn.sigmoid(a) * b).astype(BF16)
    ff = [dot(u_ref[p, :], wf_ref[...]) for p in parts]
    for i, p in enumerate(parts):
        o_ref[p, :] = h[i] + _rms(ff[i], gpost_ref[...])


def _post(yret, ysb, proj2d, x2d, wr, ws, wo, gmix, gpre, wi, wf, gpost, tm=512):
    rows = x2d.shape[0]
    row_block = lambda width, col=0: pl.BlockSpec((tm, width), lambda i: (i, col))
    return pl.pallas_call(
        _post_kernel,
        grid=(rows // tm,),
        in_specs=[
            row_block(RET_HEADS * RET_DV),
            row_block(D_MODEL),
            row_block(D_MODEL, (OFF_GA - RET_COLS) // D_MODEL),
            row_block(D_MODEL, (OFF_GB - RET_COLS) // D_MODEL),
            row_block(D_MODEL),
            _resident(wr.shape), _resident(ws.shape), _resident(wo.shape),
            _resident((1, D_MODEL)), _resident((1, D_MODEL)),
            _resident(wi.shape), _resident(wf.shape),
            _resident((1, D_MODEL)),
        ],
        out_specs=row_block(D_MODEL),
        out_shape=jax.ShapeDtypeStruct((rows, D_MODEL), F32),
        scratch_shapes=[pltpu.VMEM((tm, D_FF), BF16)],
        compiler_params=pltpu.CompilerParams(
            dimension_semantics=("parallel",),
            vmem_limit_bytes=VMEM_LIMIT),
        name="post",
    )(yret, ysb, proj2d, proj2d, x2d, wr, ws, wo, gmix, gpre, wi, wf, gpost)


def kernel(x, meta_tokens, w_in, w_ret_out, w_sb_out, w_out, w_ffn_in, w_ffn_out,
           norm_mix_pre, norm_mix_post, norm_ffn_pre, norm_ffn_post):
    batch, seq, d = x.shape
    assert d == D_MODEL and seq % 1024 == 0 and w_in.shape[0] == 1
    x2d = x.reshape(batch * seq, d)
    meta_chunk = jnp.concatenate(
        [jnp.zeros((PAD, d), x.dtype), meta_tokens.astype(x.dtype)], axis=0)

    later = [w_ret_out[0], w_sb_out[0], w_out[0], w_ffn_in[0], w_ffn_out[0]]
    proj2d, mproj, yret, *later_bf16 = _front(
        x2d, meta_chunk, norm_mix_pre, w_in[0].astype(BF16), seq,
        [w.reshape(D_MODEL, -1) for w in later])
    wr, ws, wo, wi, wf = [c.reshape(w.shape) for c, w in zip(later_bf16, later)]
    ysb = _stickbreak(proj2d.reshape(batch, seq, OUT_COLS), mproj, seq)

    out = _post(yret, ysb.reshape(batch * seq, -1), proj2d, x2d, wr, ws, wo, norm_mix_post,
                norm_ffn_pre, wi, wf, norm_ffn_post)
    return out.reshape(batch, seq, d)
```

```python
import functools

import jax
import jax.numpy as jnp
import numpy as np
from jax import lax
from jax.experimental import pallas as pl
from jax.experimental.pallas import tpu as pltpu

D_MODEL = 1024
N_META = 16
CHUNK = 128
PAD = CHUNK - N_META
RET_HEADS = 4
RET_DK = 256
RET_DV = 512
SB_HEADS = 16
SB_DH = 64
D_FF = 2816
ROPE_BASE = 10000.0
NORM_EPS = 1e-6
GN_EPS = 1e-5

OFF_RQ = 0
OFF_RK = 1024
OFF_RV = 2048
OFF_RG = 4096
OFF_SQ = 6144
OFF_SK = 7168
OFF_SV = 8192
OFF_GA = 9216
OFF_GB = 10240
PROJ_WIDTH = 11264

F32 = jnp.float32
BF16 = jnp.bfloat16

LOG2E = 1.4426950408889634
SB_STOP = 127.0
SB_WIN = 2 * CHUNK

VMEM_LIMIT = 60 * 1024 * 1024


def _rms(xf, g):
    ms = jnp.mean(xf * xf, axis=-1, keepdims=True)
    return xf * lax.rsqrt(ms + NORM_EPS) * g


def _resident(shape):
    return pl.BlockSpec(shape, lambda i: (0,) * len(shape), pipeline_mode=pl.Buffered(1))


FRONT_ROWS = 256
FRONT_COLS = 1024
RET_COLS = OFF_SQ
OUT_COLS = PROJ_WIDTH - RET_COLS


def _rotate(x, cos, sin):
    half = RET_DK // 2
    x1, x2 = x[:, :half], x[:, half:]
    return jnp.concatenate([x1 * cos - x2 * sin, x1 * sin + x2 * cos], axis=1)


def _kv_state(k_rot, v, zeta):
    kz = (k_rot * zeta).astype(BF16)
    return lax.dot_general(kz, v, (((0,), (0,)), ((), ())), preferred_element_type=F32)


def _front_kernel(tiles_per_batch, n_cast,
                  x_ref, m_ref, g_ref, w_ref, cbase_ref, sbase_ref, coff_ref, soff_ref,
                  mcos_ref, msin_ref, mzeta_ref, decay_ref, xi_ref, zeta_ref, gchunk_ref, *rest):
    cast_in, (o_ref, mo_ref, y_ref), rest = rest[:n_cast], rest[n_cast:n_cast + 3], rest[n_cast + 3:]
    cast_out, (ring_ref, state_ref, mret_ref) = rest[:n_cast], rest[n_cast:]
    j = pl.program_id(0)

    for src, dst in zip(cast_in, cast_out):
        dst[...] = src[0].astype(BF16)
    cur = j % 2
    prev = 1 - cur
    n_col = PROJ_WIDTH // FRONT_COLS
    n_ret = RET_COLS // FRONT_COLS

    def normed(rows):
        return _rms(rows, g_ref[...]).astype(BF16)

    def project(hn, put):
        for t in range(n_col):
            cols = slice(t * FRONT_COLS, (t + 1) * FRONT_COLS)
            put(t, jnp.dot(hn, w_ref[:, cols], preferred_element_type=F32).astype(BF16))
            yield t

    @pl.when(j == 0)
    def _():
        ring_ref[1] = jnp.zeros(ring_ref.shape[1:], BF16)
        state_ref[...] = jnp.zeros(state_ref.shape, F32)

        def put_meta(t, res):
            lo = t * FRONT_COLS
            if OFF_RK <= lo < OFF_RG:
                mret_ref[:, lo - OFF_RK:lo - OFF_RK + FRONT_COLS] = res
            elif lo >= RET_COLS:
                mo_ref[:, lo - RET_COLS:lo - RET_COLS + FRONT_COLS] = res
        for _ in project(normed(m_ref[...]), put_meta):
            pass

    @pl.when((j - 1) % tiles_per_batch == 0)
    def _():
        mcos, msin = mcos_ref[...], msin_ref[...]
        for h in range(RET_HEADS):
            k = mret_ref[:, h * RET_DK:(h + 1) * RET_DK].astype(F32)
            v = mret_ref[:, OFF_RV - OFF_RK + h * RET_DV:OFF_RV - OFF_RK + (h + 1) * RET_DV]
            state_ref[h] = _kv_state(_rotate(k, mcos, msin), v, mzeta_ref[h])

    co, so = coff_ref[...], soff_ref[...]
    state = [state_ref[h] for h in range(RET_HEADS)]
    trig = {}
    held = {}

    def phase1(h):
        if not trig:
            cb, sb = cbase_ref[0], sbase_ref[0]
            trig[0] = (cb * co - sb * so, sb * co + cb * so)
        cos, sin = trig[0]
        q = ring_ref[prev, :, OFF_RQ + h * RET_DK:OFF_RQ + (h + 1) * RET_DK].astype(F32)
        k = ring_ref[prev, :, OFF_RK + h * RET_DK:OFF_RK + (h + 1) * RET_DK].astype(F32)
        v = ring_ref[prev, :, OFF_RV + h * RET_DV:OFF_RV + (h + 1) * RET_DV]
        q_rot = (_rotate(q, cos, sin) * (RET_DK ** -0.5)).astype(BF16)
        k_rot = _rotate(k, cos, sin)
        scores = lax.dot_general(q_rot, k_rot.astype(BF16), (((1,), (1,)), ((), ())),
                                 preferred_element_type=F32)
        held[h] = ((scores * decay_ref[h]).astype(BF16), q_rot, (k_rot * zeta_ref[h]).astype(BF16), v)

    def phase2(h):
        sd, q_rot, kz, v = held.pop(h)
        inner = jnp.dot(sd, v, preferred_element_type=F32)
        cross = jnp.dot(q_rot, state[h].astype(BF16), preferred_element_type=F32)
        kv = lax.dot_general(kz, v, (((0,), (0,)), ((), ())), preferred_element_type=F32)
        state[h] = gchunk_ref[h, 0:1, :] * state[h] + kv
        y = inner + cross * xi_ref[h]
        mu = jnp.mean(y, axis=-1, keepdims=True)
        yc = y - mu
        var = jnp.mean(yc * yc, axis=-1, keepdims=True)
        yn = yc * lax.rsqrt(var + GN_EPS)
        gate = ring_ref[prev, :, OFF_RG + h * RET_DV:OFF_RG + (h + 1) * RET_DV].astype(F32)
        y_ref[:, h * RET_DV:(h + 1) * RET_DV] = (gate * jax.nn.sigmoid(gate) * yn).astype(BF16)

    def put(t, res):
        lo = t * FRONT_COLS
        if t < n_ret:
            ring_ref[cur, :, lo:lo + FRONT_COLS] = res
        else:
            o_ref[:, lo - RET_COLS:lo - RET_COLS + FRONT_COLS] = res

    for t in project(normed(x_ref[...]), put):
        if t < 2 * RET_HEADS:
            (phase1 if t % 2 == 0 else phase2)(t // 2)
    assert 2 * RET_HEADS <= n_col

    for h in range(RET_HEADS):
        state_ref[h] = state[h]


def _retention_tables(chunk):
    hh = np.arange(RET_HEADS, dtype=np.float64)
    log_g = np.log1p(-(2.0 ** (-5.0 - hh)))
    idx = np.arange(chunk, dtype=np.float64)
    diff = idx[:, None] - idx[None, :]
    decay = np.where(diff >= 0, np.exp(log_g[:, None, None] * np.maximum(diff, 0.0)), 0.0)
    zeta = np.exp(log_g[:, None] * (chunk - 1.0 - idx))[:, :, None]
    xi = np.exp(log_g[:, None] * (idx + 1.0))[:, :, None]
    gchunk = np.exp(log_g * chunk)[:, None, None]
    bc = lambda a, shape: jnp.asarray(np.broadcast_to(a, shape), F32)
    return (bc(decay, (RET_HEADS, chunk, chunk)),
            bc(xi, (RET_HEADS, chunk, RET_DV)),
            bc(zeta, (RET_HEADS, chunk, RET_DK)),
            bc(gchunk, (RET_HEADS, 8, RET_DV)))


def _rope_tables(seq, chunk):
    half = RET_DK // 2
    inv = ROPE_BASE ** (-np.arange(half, dtype=np.float64) / half)
    base = (N_META + chunk * np.arange(seq // chunk, dtype=np.float64))[:, None, None] * inv
    off = np.arange(chunk, dtype=np.float64)[:, None] * inv
    meta = (np.arange(CHUNK, dtype=np.float64) - PAD)[:, None] * inv
    f = lambda a: jnp.asarray(a, F32)
    return (f(np.cos(base)), f(np.sin(base)), f(np.cos(off)), f(np.sin(off)),
            f(np.cos(meta)), f(np.sin(meta)))


def _cast_slab(rows, steps):
    slab = next(s for s in range(16, rows + 1, 16) if rows % s == 0 and rows // s <= steps)
    return slab, rows // slab


def _front(x2d, meta_chunk, gain, w_bf16, seq, to_cast):
    rows = x2d.shape[0]
    tm = FRONT_ROWS
    n_tiles = rows // tm
    tiles_per_batch = seq // tm
    cbase, sbase, coff, soff, mcos, msin = _rope_tables(seq, tm)
    decay, xi, zeta, gchunk = _retention_tables(tm)
    mzeta = _retention_tables(CHUNK)[2]
    this_tile = lambda j: (jnp.minimum(j, n_tiles - 1), 0)
    last_tile = lambda j: (jnp.maximum(j - 1, 0), 0)
    last_base = lambda j: (jnp.maximum(j - 1, 0) % tiles_per_batch, 0, 0)
    cast_in_specs, cast_out_specs = [], []
    for w in to_cast:
        slab, blocks = _cast_slab(w.shape[1], n_tiles)
        cast_in_specs.append(pl.BlockSpec(
            (1, slab, w.shape[2]), lambda j, blocks=blocks: (0, jnp.minimum(j, blocks - 1), 0)))
        cast_out_specs.append(pl.BlockSpec(
            (slab, w.shape[2]), lambda j, blocks=blocks: (jnp.minimum(j, blocks - 1), 0)))
    return pl.pallas_call(
        functools.partial(_front_kernel, tiles_per_batch, len(to_cast)),
        grid=(n_tiles + 1,),
        in_specs=[
            pl.BlockSpec((tm, D_MODEL), this_tile),
            _resident((CHUNK, D_MODEL)),
            _resident((1, D_MODEL)),
            _resident((D_MODEL, PROJ_WIDTH)),
            pl.BlockSpec((1, 1, RET_DK // 2), last_base),
            pl.BlockSpec((1, 1, RET_DK // 2), last_base),
            _resident((tm, RET_DK // 2)), _resident((tm, RET_DK // 2)),
            _resident((CHUNK, RET_DK // 2)), _resident((CHUNK, RET_DK // 2)),
            _resident((RET_HEADS, CHUNK, RET_DK)),
            _resident((RET_HEADS, tm, tm)),
            _resident((RET_HEADS, tm, RET_DV)),
            _resident((RET_HEADS, tm, RET_DK)),
            _resident((RET_HEADS, 8, RET_DV)),
        ] + cast_in_specs,
        out_specs=[
            pl.BlockSpec((tm, OUT_COLS), this_tile),
            pl.BlockSpec((CHUNK, OUT_COLS), lambda j: (0, 0)),
            pl.BlockSpec((tm, RET_HEADS * RET_DV), last_tile),
        ] + cast_out_specs,
        out_shape=[
            jax.ShapeDtypeStruct((rows, OUT_COLS), BF16),
            jax.ShapeDtypeStruct((CHUNK, OUT_COLS), BF16),
            jax.ShapeDtypeStruct((rows, RET_HEADS * RET_DV), BF16),
        ] + [jax.ShapeDtypeStruct(w.shape[1:], BF16) for w in to_cast],
        scratch_shapes=[
            pltpu.VMEM((2, tm, RET_COLS), BF16),
            pltpu.VMEM((RET_HEADS, RET_DK, RET_DV), F32),
            pltpu.VMEM((CHUNK, OFF_RG - OFF_RK), BF16),
        ],
        compiler_params=pltpu.CompilerParams(
            dimension_semantics=("arbitrary",),
            vmem_limit_bytes=VMEM_LIMIT),
        name="front",
    )(x2d, meta_chunk, gain, w_bf16, cbase, sbase, coff, soff, mcos, msin, mzeta, decay, xi, zeta, gchunk,
      *to_cast)


SB_GROUP = 9


def _softplus2(z):
    return jnp.maximum(z, jnp.log2(1.0 + jnp.exp2(jnp.minimum(z, 126.0))))


def _sb_kernel(q_ref, k_ref, v_ref, mk_ref, mv_ref, tri_ref, o_ref, r_ref):
    nq = q_ref.shape[1] // CHUNK
    lanes = 2 * SB_DH
    q_rows = lambda start, size: q_ref[0, pl.ds(start, size), :]
    k_rows = lambda start, size: k_ref[0, pl.ds(start, size), :]
    v_rows = lambda start, size: v_ref[0, pl.ds(start, size), :]
    meta_k = lambda: mk_ref[...]
    meta_v = lambda: mv_ref[...]
    head0_q = lax.broadcasted_iota(jnp.int32, (CHUNK, lanes), 1) < SB_DH
    head0_kv = lax.broadcasted_iota(jnp.int32, (SB_WIN, lanes), 1) < SB_DH
    t_loc = lax.broadcasted_iota(jnp.int32, (2 * CHUNK, CHUNK), 0) & (CHUNK - 1)
    s_loc = lax.broadcasted_iota(jnp.int32, (2 * CHUNK, CHUNK), 1)
    causal = s_loc < t_loc
    tri = tri_ref[...]

    def stacked_q(row0):
        qf = q_rows(row0, CHUNK).astype(F32) * (SB_DH ** -0.5 * LOG2E)
        both = jnp.concatenate([jnp.where(head0_q, qf, 0.0), jnp.where(head0_q, 0.0, qf)], axis=0)
        return both.astype(BF16)

    mask = lambda x, m: x if m is None else jnp.where(m, x, 0.0)

    def scores(qs, kw):
        return lax.dot_general(qs, kw, (((1,), (1,)), ((), ())), preferred_element_type=F32)

    def log_sticks(z, valid_lo, valid_hi):
        sp = _softplus2(z)
        ln = jnp.concatenate([mask(sp[:, :CHUNK], valid_lo), mask(sp[:, CHUNK:], valid_hi)], axis=1)
        return z - sp, ln[:, 0:1], ln.astype(BF16)

    def later_sums(ln):
        return jnp.dot(ln, tri, preferred_element_type=F32)

    def weights(log_beta, ln0, later, valid_lo, valid_hi, r):
        total = later[:, 0:1] + ln0
        if r is not None:
            later = later + r
            total = total + r
        a = jnp.exp2(log_beta - later)
        a = jnp.concatenate([mask(a[:, :CHUNK], valid_lo), mask(a[:, CHUNK:], valid_hi)], axis=1)
        return a.astype(BF16), total

    def attend(a, vw):
        zero = jnp.zeros_like(vw)
        return (jnp.dot(a[:CHUNK], jnp.where(head0_kv, vw, zero), preferred_element_type=F32)
                + jnp.dot(a[CHUNK:], jnp.where(head0_kv, zero, vw), preferred_element_type=F32))

    def windows(qs, kw, vw, valid_lo, valid_hi, r):
        n = len(qs)
        z, st, later, rs, outs = {}, {}, {}, [None] * n, [None] * n
        for t in range(n + 3):
            if t < n:
                z[t] = scores(qs[t], kw[t])
            if 0 <= t - 1 < n:
                st[t - 1] = log_sticks(z.pop(t - 1), valid_lo, valid_hi)
            if 0 <= t - 2 < n:
                later[t - 2] = later_sums(st[t - 2][2])
            if 0 <= t - 3 < n:
                u = t - 3
                a, rs[u] = weights(st[u][0], st[u][1], later.pop(u), valid_lo, valid_hi, r[u])
                outs[u] = attend(a, vw[u])
        return rs, outs

    def diagonal_windows(blocks, first):
        row0 = [pl.multiple_of(i * CHUNK, CHUNK) for i in blocks]
        qs = [stacked_q(r0) for r0 in row0]
        if first:
            kw = [jnp.concatenate([meta_k(), k_rows(0, CHUNK)], axis=0)]
            vw = [jnp.concatenate([meta_v(), v_rows(0, CHUNK)], axis=0)]
            valid_lo = s_loc >= PAD
        else:
            start = [pl.multiple_of(r0 - CHUNK, CHUNK) for r0 in row0]
            kw = [k_rows(s, SB_WIN) for s in start]
            vw = [v_rows(s, SB_WIN) for s in start]
            valid_lo = None
        r, out = windows(qs, kw, vw, valid_lo, causal, [None] * len(blocks))
        return qs, r, out

    def load_block(rows, meta, j):
        start = pl.multiple_of(jnp.maximum(j, 0) * CHUNK, CHUNK)
        return jnp.where(j < 0, meta(), rows(start, CHUNK))

    def earlier_windows(i, qs, r, acc):
        def window(state):
            j_hi, r, acc = state
            j_lo = j_hi - 1
            kw = jnp.concatenate([load_block(k_rows, meta_k, j_lo), load_block(k_rows, meta_k, j_hi)], axis=0)
            vw = jnp.concatenate([load_block(v_rows, meta_v, j_lo), load_block(v_rows, meta_v, j_hi)], axis=0)
            valid_lo = j_lo * CHUNK + s_loc >= -N_META
            valid_hi = j_hi * CHUNK + s_loc >= -N_META
            r, out = windows([qs], [kw], [vw], valid_lo, valid_hi, [r])
            return j_hi - 2, r[0], acc + out[0]

        def more(state):
            j_hi, r, _ = state
            return (j_hi >= -1) & (jnp.min(r) < SB_STOP)

        return lax.while_loop(more, window, (i - 2, r, acc))[2]

    _, _, out = diagonal_windows([0], True)
    o_ref[0, 0:CHUNK, :] = out[0].astype(BF16)

    def store(i, acc):
        o_ref[0, pl.ds(pl.multiple_of(i * CHUNK, CHUNK), CHUNK), :] = acc.astype(BF16)

    def finish_group(g):
        def block(u, carry):
            r = r_ref[g & 1, u]

            @pl.when(jnp.min(r) < SB_STOP)
            def _():
                i = 1 + g * SB_GROUP + u
                row0 = pl.multiple_of(i * CHUNK, CHUNK)
                acc = o_ref[0, pl.ds(row0, CHUNK), :].astype(F32)
                store(i, earlier_windows(i, stacked_q(row0), r, acc))
            return carry
        lax.fori_loop(0, SB_GROUP, block, 0)

    def group(g, r_prev):
        prev_unfinished = jnp.min(r_prev) < SB_STOP
        blocks = [1 + g * SB_GROUP + u for u in range(SB_GROUP)]
        _, r, out = diagonal_windows(blocks, False)
        for u, i in enumerate(blocks):
            store(i, out[u])
            r_ref[g & 1, u] = r[u]

        @pl.when(prev_unfinished)
        def _():
            finish_group(g - 1)

        return functools.reduce(jnp.minimum, r)

    assert (nq - 1) % SB_GROUP == 0
    n_groups = (nq - 1) // SB_GROUP
    r_last = lax.fori_loop(0, n_groups, group, jnp.full((2 * CHUNK, 1), SB_STOP, F32))

    @pl.when(jnp.min(r_last) < SB_STOP)
    def _():
        finish_group(n_groups - 1)


def _sb_tables():
    j = np.arange(SB_WIN)[:, None]
    s = np.arange(SB_WIN)[None, :]
    return jnp.asarray((j > s).astype(np.float32), BF16)


def _stickbreak(proj, mproj, seq):
    batch = proj.shape[0]
    lanes = 2 * SB_DH
    pairs = SB_HEADS // 2
    return pl.pallas_call(
        _sb_kernel,
        grid=(batch, pairs),
        in_specs=[
            pl.BlockSpec((1, seq, lanes), lambda b, p: (b, 0, (OFF_SQ - RET_COLS) // lanes + p)),
            pl.BlockSpec((1, seq, lanes), lambda b, p: (b, 0, (OFF_SK - RET_COLS) // lanes + p)),
            pl.BlockSpec((1, seq, lanes), lambda b, p: (b, 0, (OFF_SV - RET_COLS) // lanes + p)),
            pl.BlockSpec((CHUNK, lanes), lambda b, p: (0, (OFF_SK - RET_COLS) // lanes + p)),
            pl.BlockSpec((CHUNK, lanes), lambda b, p: (0, (OFF_SV - RET_COLS) // lanes + p)),
            pl.BlockSpec((SB_WIN, SB_WIN), lambda b, p: (0, 0)),
        ],
        out_specs=pl.BlockSpec((1, seq, lanes), lambda b, p: (b, 0, p)),
        out_shape=jax.ShapeDtypeStruct((batch, seq, SB_HEADS * SB_DH), BF16),
        scratch_shapes=[pltpu.VMEM((2, SB_GROUP, 2 * CHUNK, 1), F32)],
        compiler_params=pltpu.CompilerParams(
            dimension_semantics=("parallel", "parallel"),
            vmem_limit_bytes=VMEM_LIMIT),
        name="stickbreak",
    )(proj, proj, proj, mproj, mproj, _sb_tables())


FFN_COLS = 256
POST_PARTS = 2


def _post_kernel(yr_ref, ys_ref, ga_ref, gb_ref, x_ref, wr_ref, ws_ref, wo_ref, gmix_ref,
                 gpre_ref, wi_ref, wf_ref, gpost_ref, o_ref, u_ref):
    tm = o_ref.shape[0]
    parts = [slice(p * tm // POST_PARTS, (p + 1) * tm // POST_PARTS) for p in range(POST_PARTS)]
    dot = functools.partial(jnp.dot, preferred_element_type=F32)

    r = [dot(yr_ref[p, :], wr_ref[...]) for p in parts]
    s = [dot(ys_ref[p, :], ws_ref[...]) for p in parts]
    mix = []
    for i, p in enumerate(parts):
        merged = (jax.nn.sigmoid(ga_ref[p, :].astype(F32)) * r[i]
                  + jax.nn.sigmoid(gb_ref[p, :].astype(F32)) * s[i])
        mix.append(dot(merged.astype(BF16), wo_ref[...]))
    h, hn = [], []
    for i, p in enumerate(parts):
        h.append(x_ref[p, :] + _rms(mix[i], gmix_ref[...]))
        hn.append(_rms(h[i], gpre_ref[...]).astype(BF16))
    for c in range(D_FF // FFN_COLS):
        lo = c * FFN_COLS
        for i, p in enumerate(parts):
            a = dot(hn[i], wi_ref[:, lo:lo + FFN_COLS])
            b = dot(hn[i], wi_ref[:, D_FF + lo:D_FF + lo + FFN_COLS])
            u_ref[p, lo:lo + FFN_COLS] = (a * jax.nn.sigmoid(a) * b).astype(BF16)
    ff = [dot(u_ref[p, :], wf_ref[...]) for p in parts]
    for i, p in enumerate(parts):
        o_ref[p, :] = h[i] + _rms(ff[i], gpost_ref[...])


def _post(yret, ysb, proj2d, x2d, wr, ws, wo, gmix, gpre, wi, wf, gpost, tm=512):
    rows = x2d.shape[0]
    row_block = lambda width, col=0: pl.BlockSpec((tm, width), lambda i: (i, col))
    return pl.pallas_call(
        _post_kernel,
        grid=(rows // tm,),
        in_specs=[
            row_block(RET_HEADS * RET_DV),
            row_block(D_MODEL),
            row_block(D_MODEL, (OFF_GA - RET_COLS) // D_MODEL),
            row_block(D_MODEL, (OFF_GB - RET_COLS) // D_MODEL),
            row_block(D_MODEL),
            _resident(wr.shape), _resident(ws.shape), _resident(wo.shape),
            _resident((1, D_MODEL)), _resident((1, D_MODEL)),
            _resident(wi.shape), _resident(wf.shape),
            _resident((1, D_MODEL)),
        ],
        out_specs=row_block(D_MODEL),
        out_shape=jax.ShapeDtypeStruct((rows, D_MODEL), F32),
        scratch_shapes=[pltpu.VMEM((tm, D_FF), BF16)],
        compiler_params=pltpu.CompilerParams(
            dimension_semantics=("parallel",),
            vmem_limit_bytes=VMEM_LIMIT),
        name="post",
    )(yret, ysb, proj2d, proj2d, x2d, wr, ws, wo, gmix, gpre, wi, wf, gpost)


def kernel(x, meta_tokens, w_in, w_ret_out, w_sb_out, w_out, w_ffn_in, w_ffn_out,
           norm_mix_pre, norm_mix_post, norm_ffn_pre, norm_ffn_post):
    batch, seq, d = x.shape
    assert d == D_MODEL and seq % 1024 == 0 and w_in.shape[0] == 1
    x2d = x.reshape(batch * seq, d)
    meta_chunk = jnp.concatenate(
        [jnp.zeros((PAD, d), x.dtype), meta_tokens.astype(x.dtype)], axis=0)

    proj2d, mproj, yret, wr, ws, wo, wi, wf = _front(
        x2d, meta_chunk, norm_mix_pre, w_in[0].astype(BF16), seq,
        [w_ret_out, w_sb_out, w_out, w_ffn_in, w_ffn_out])
    ysb = _stickbreak(proj2d.reshape(batch, seq, OUT_COLS), mproj, seq)

    out = _post(yret, ysb.reshape(batch * seq, -1), proj2d, x2d, wr, ws, wo, norm_mix_post,
                norm_ffn_pre, wi, wf, norm_ffn_post)
    return out.reshape(batch, seq, d)
```

```python
import functools

import jax
import jax.numpy as jnp
import numpy as np
from jax import lax
from jax.experimental import pallas as pl
from jax.experimental.pallas import tpu as pltpu

D_MODEL = 1024
N_META = 16
CHUNK = 128
PAD = CHUNK - N_META
RET_HEADS = 4
RET_DK = 256
RET_DV = 512
SB_HEADS = 16
SB_DH = 64
D_FF = 2816
ROPE_BASE = 10000.0
NORM_EPS = 1e-6
GN_EPS = 1e-5

OFF_RQ = 0
OFF_RK = 1024
OFF_RV = 2048
OFF_RG = 4096
OFF_SQ = 6144
OFF_SK = 7168
OFF_SV = 8192
OFF_GA = 9216
OFF_GB = 10240
PROJ_WIDTH = 11264

F32 = jnp.float32
BF16 = jnp.bfloat16

LOG2E = 1.4426950408889634
SB_STOP = 127.0
SB_WIN = 2 * CHUNK

VMEM_LIMIT = 60 * 1024 * 1024


def _rms(xf, g):
    ms = jnp.mean(xf * xf, axis=-1, keepdims=True)
    return xf * lax.rsqrt(ms + NORM_EPS) * g


def _resident(shape):
    return pl.BlockSpec(shape, lambda i: (0,) * len(shape), pipeline_mode=pl.Buffered(1))


FRONT_ROWS = 256
FRONT_COLS = 1024
RET_COLS = OFF_SQ
OUT_COLS = PROJ_WIDTH - RET_COLS


def _rotate(x, cos, sin):
    half = RET_DK // 2
    x1, x2 = x[:, :half], x[:, half:]
    return jnp.concatenate([x1 * cos - x2 * sin, x1 * sin + x2 * cos], axis=1)


def _kv_state(k_rot, v, zeta):
    kz = (k_rot * zeta).astype(BF16)
    return lax.dot_general(kz, v, (((0,), (0,)), ((), ())), preferred_element_type=F32)


def _front_kernel(tiles_per_batch, n_cast,
                  x_ref, m_ref, g_ref, w_ref, cbase_ref, sbase_ref, coff_ref, soff_ref,
                  mcos_ref, msin_ref, mzeta_ref, decay_ref, xi_ref, zeta_ref, gchunk_ref, *rest):
    cast_in, (o_ref, mo_ref, y_ref), rest = rest[:n_cast], rest[n_cast:n_cast + 3], rest[n_cast + 3:]
    cast_out, (ring_ref, state_ref, mret_ref) = rest[:n_cast], rest[n_cast:]
    j = pl.program_id(0)

    for src, dst in zip(cast_in, cast_out):
        dst[...] = src[0].astype(BF16)
    cur = j % 2
    prev = 1 - cur
    n_col = PROJ_WIDTH // FRONT_COLS
    n_ret = RET_COLS // FRONT_COLS

    def normed(rows):
        return _rms(rows, g_ref[...]).astype(BF16)

    def project(hn, put):
        for t in range(n_col):
            cols = slice(t * FRONT_COLS, (t + 1) * FRONT_COLS)
            put(t, jnp.dot(hn, w_ref[:, cols], preferred_element_type=F32).astype(BF16))
            yield t

    @pl.when(j == 0)
    def _():
        ring_ref[1] = jnp.zeros(ring_ref.shape[1:], BF16)
        state_ref[...] = jnp.zeros(state_ref.shape, F32)

        def put_meta(t, res):
            lo = t * FRONT_COLS
            if OFF_RK <= lo < OFF_RG:
                mret_ref[:, lo - OFF_RK:lo - OFF_RK + FRONT_COLS] = res
            elif lo >= RET_COLS:
                mo_ref[:, lo - RET_COLS:lo - RET_COLS + FRONT_COLS] = res
        for _ in project(normed(m_ref[...]), put_meta):
            pass

    @pl.when((j - 1) % tiles_per_batch == 0)
    def _():
        mcos, msin = mcos_ref[...], msin_ref[...]
        for h in range(RET_HEADS):
            k = mret_ref[:, h * RET_DK:(h + 1) * RET_DK].astype(F32)
            v = mret_ref[:, OFF_RV - OFF_RK + h * RET_DV:OFF_RV - OFF_RK + (h + 1) * RET_DV]
            state_ref[h] = _kv_state(_rotate(k, mcos, msin), v, mzeta_ref[h])

    co, so = coff_ref[...], soff_ref[...]
    state = [state_ref[h] for h in range(RET_HEADS)]
    trig = {}
    held = {}

    def phase1(h):
        if not trig:
            cb, sb = cbase_ref[0], sbase_ref[0]
            trig[0] = (cb * co - sb * so, sb * co + cb * so)
        cos, sin = trig[0]
        q = ring_ref[prev, :, OFF_RQ + h * RET_DK:OFF_RQ + (h + 1) * RET_DK].astype(F32)
        k = ring_ref[prev, :, OFF_RK + h * RET_DK:OFF_RK + (h + 1) * RET_DK].astype(F32)
        v = ring_ref[prev, :, OFF_RV + h * RET_DV:OFF_RV + (h + 1) * RET_DV]
        q_rot = (_rotate(q, cos, sin) * (RET_DK ** -0.5)).astype(BF16)
        k_rot = _rotate(k, cos, sin)
        scores = lax.dot_general(q_rot, k_rot.astype(BF16), (((1,), (1,)), ((), ())),
                                 preferred_element_type=F32)
        held[h] = ((scores * decay_ref[h]).astype(BF16), q_rot, (k_rot * zeta_ref[h]).astype(BF16), v)

    def phase2(h):
        sd, q_rot, kz, v = held.pop(h)
        inner = jnp.dot(sd, v, preferred_element_type=F32)
        cross = jnp.dot(q_rot, state[h].astype(BF16), preferred_element_type=F32)
        kv = lax.dot_general(kz, v, (((0,), (0,)), ((), ())), preferred_element_type=F32)
        state[h] = gchunk_ref[h, 0:1, :] * state[h] + kv
        y = inner + cross * xi_ref[h]
        mu = jnp.mean(y, axis=-1, keepdims=True)
        yc = y - mu
        var = jnp.mean(yc * yc, axis=-1, keepdims=True)
        yn = yc * lax.rsqrt(var + GN_EPS)
        gate = ring_ref[prev, :, OFF_RG + h * RET_DV:OFF_RG + (h + 1) * RET_DV].astype(F32)
        y_ref[:, h * RET_DV:(h + 1) * RET_DV] = (gate * jax.nn.sigmoid(gate) * yn).astype(BF16)

    def put(t, res):
        lo = t * FRONT_COLS
        if t < n_ret:
            ring_ref[cur, :, lo:lo + FRONT_COLS] = res
        else:
            o_ref[:, lo - RET_COLS:lo - RET_COLS + FRONT_COLS] = res

    for t in project(normed(x_ref[...]), put):
        if t < 2 * RET_HEADS:
            (phase1 if t % 2 == 0 else phase2)(t // 2)
    assert 2 * RET_HEADS <= n_col

    for h in range(RET_HEADS):
        state_ref[h] = state[h]


def _retention_tables(chunk):
    hh = np.arange(RET_HEADS, dtype=np.float64)
    log_g = np.log1p(-(2.0 ** (-5.0 - hh)))
    idx = np.arange(chunk, dtype=np.float64)
    diff = idx[:, None] - idx[None, :]
    decay = np.where(diff >= 0, np.exp(log_g[:, None, None] * np.maximum(diff, 0.0)), 0.0)
    zeta = np.exp(log_g[:, None] * (chunk - 1.0 - idx))[:, :, None]
    xi = np.exp(log_g[:, None] * (idx + 1.0))[:, :, None]
    gchunk = np.exp(log_g * chunk)[:, None, None]
    bc = lambda a, shape: jnp.asarray(np.broadcast_to(a, shape), F32)
    return (bc(decay, (RET_HEADS, chunk, chunk)),
            bc(xi, (RET_HEADS, chunk, RET_DV)),
            bc(zeta, (RET_HEADS, chunk, RET_DK)),
            bc(gchunk, (RET_HEADS, 8, RET_DV)))


def _rope_tables(seq, chunk):
    half = RET_DK // 2
    inv = ROPE_BASE ** (-np.arange(half, dtype=np.float64) / half)
    base = (N_META + chunk * np.arange(seq // chunk, dtype=np.float64))[:, None, None] * inv
    off = np.arange(chunk, dtype=np.float64)[:, None] * inv
    meta = (np.arange(CHUNK, dtype=np.float64) - PAD)[:, None] * inv
    f = lambda a: jnp.asarray(a, F32)
    return (f(np.cos(base)), f(np.sin(base)), f(np.cos(off)), f(np.sin(off)),
            f(np.cos(meta)), f(np.sin(meta)))


def _cast_slab(rows, steps):
    slab = next(s for s in range(16, rows + 1, 16) if rows % s == 0 and rows // s <= steps)
    return slab, rows // slab


def _front(x2d, meta_chunk, gain, w_bf16, seq, to_cast):
    rows = x2d.shape[0]
    tm = FRONT_ROWS
    n_tiles = rows // tm
    tiles_per_batch = seq // tm
    cbase, sbase, coff, soff, mcos, msin = _rope_tables(seq, tm)
    decay, xi, zeta, gchunk = _retention_tables(tm)
    mzeta = _retention_tables(CHUNK)[2]
    this_tile = lambda j: (jnp.minimum(j, n_tiles - 1), 0)
    last_tile = lambda j: (jnp.maximum(j - 1, 0), 0)
    last_base = lambda j: (jnp.maximum(j - 1, 0) % tiles_per_batch, 0, 0)
    cast_in_specs, cast_out_specs = [], []
    for w in to_cast:
        slab, blocks = _cast_slab(w.shape[1], n_tiles)
        cast_in_specs.append(pl.BlockSpec(
            (1, slab, w.shape[2]), lambda j, blocks=blocks: (0, jnp.minimum(j, blocks - 1), 0)))
        cast_out_specs.append(pl.BlockSpec(
            (slab, w.shape[2]), lambda j, blocks=blocks: (jnp.minimum(j, blocks - 1), 0)))
    return pl.pallas_call(
        functools.partial(_front_kernel, tiles_per_batch, len(to_cast)),
        grid=(n_tiles + 1,),
        in_specs=[
            pl.BlockSpec((tm, D_MODEL), this_tile),
            _resident((CHUNK, D_MODEL)),
            _resident((1, D_MODEL)),
            _resident((D_MODEL, PROJ_WIDTH)),
            pl.BlockSpec((1, 1, RET_DK // 2), last_base),
            pl.BlockSpec((1, 1, RET_DK // 2), last_base),
            _resident((tm, RET_DK // 2)), _resident((tm, RET_DK // 2)),
            _resident((CHUNK, RET_DK // 2)), _resident((CHUNK, RET_DK // 2)),
            _resident((RET_HEADS, CHUNK, RET_DK)),
            _resident((RET_HEADS, tm, tm)),
            _resident((RET_HEADS, tm, RET_DV)),
            _resident((RET_HEADS, tm, RET_DK)),
            _resident((RET_HEADS, 8, RET_DV)),
        ] + cast_in_specs,
        out_specs=[
            pl.BlockSpec((tm, OUT_COLS), this_tile),
            pl.BlockSpec((CHUNK, OUT_COLS), lambda j: (0, 0)),
            pl.BlockSpec((tm, RET_HEADS * RET_DV), last_tile),
        ] + cast_out_specs,
        out_shape=[
            jax.ShapeDtypeStruct((rows, OUT_COLS), BF16),
            jax.ShapeDtypeStruct((CHUNK, OUT_COLS), BF16),
            jax.ShapeDtypeStruct((rows, RET_HEADS * RET_DV), BF16),
        ] + [jax.ShapeDtypeStruct(w.shape[1:], BF16) for w in to_cast],
        scratch_shapes=[
            pltpu.VMEM((2, tm, RET_COLS), BF16),
            pltpu.VMEM((RET_HEADS, RET_DK, RET_DV), F32),
            pltpu.VMEM((CHUNK, OFF_RG - OFF_RK), BF16),
        ],
        compiler_params=pltpu.CompilerParams(
            dimension_semantics=("arbitrary",),
            vmem_limit_bytes=VMEM_LIMIT),
        name="front",
    )(x2d, meta_chunk, gain, w_bf16, cbase, sbase, coff, soff, mcos, msin, mzeta, decay, xi, zeta, gchunk,
      *to_cast)


SB_GROUP = 9
SB_FILL_COLS = 256


def _softplus2(z):
    return jnp.maximum(z, jnp.log2(1.0 + jnp.exp2(jnp.minimum(z, 126.0))))


def _sb_kernel(q_ref, k_ref, v_ref, mk_ref, mv_ref, tri_ref, yr_ref, wr_ref, o_ref, ro_ref, r_ref):
    nq = q_ref.shape[1] // CHUNK
    lanes = 2 * SB_DH
    q_rows = lambda start, size: q_ref[0, pl.ds(start, size), :]
    k_rows = lambda start, size: k_ref[0, pl.ds(start, size), :]
    v_rows = lambda start, size: v_ref[0, pl.ds(start, size), :]
    meta_k = lambda: mk_ref[...]
    meta_v = lambda: mv_ref[...]
    head0_q = lax.broadcasted_iota(jnp.int32, (CHUNK, lanes), 1) < SB_DH
    head0_kv = lax.broadcasted_iota(jnp.int32, (SB_WIN, lanes), 1) < SB_DH
    t_loc = lax.broadcasted_iota(jnp.int32, (2 * CHUNK, CHUNK), 0) & (CHUNK - 1)
    s_loc = lax.broadcasted_iota(jnp.int32, (2 * CHUNK, CHUNK), 1)
    causal = s_loc < t_loc
    tri = tri_ref[...]

    def stacked_q(row0):
        qf = q_rows(row0, CHUNK).astype(F32) * (SB_DH ** -0.5 * LOG2E)
        both = jnp.concatenate([jnp.where(head0_q, qf, 0.0), jnp.where(head0_q, 0.0, qf)], axis=0)
        return both.astype(BF16)

    mask = lambda x, m: x if m is None else jnp.where(m, x, 0.0)

    def scores(qs, kw):
        return lax.dot_general(qs, kw, (((1,), (1,)), ((), ())), preferred_element_type=F32)

    def log_sticks(z, valid_lo, valid_hi):
        sp = _softplus2(z)
        ln = jnp.concatenate([mask(sp[:, :CHUNK], valid_lo), mask(sp[:, CHUNK:], valid_hi)], axis=1)
        return z - sp, ln[:, 0:1], ln.astype(BF16)

    def later_sums(ln):
        return jnp.dot(ln, tri, preferred_element_type=F32)

    def weights(log_beta, ln0, later, valid_lo, valid_hi, r):
        total = later[:, 0:1] + ln0
        if r is not None:
            later = later + r
            total = total + r
        a = jnp.exp2(log_beta - later)
        a = jnp.concatenate([mask(a[:, :CHUNK], valid_lo), mask(a[:, CHUNK:], valid_hi)], axis=1)
        return a.astype(BF16), total

    def attend(a, vw):
        zero = jnp.zeros_like(vw)
        return (jnp.dot(a[:CHUNK], jnp.where(head0_kv, vw, zero), preferred_element_type=F32)
                + jnp.dot(a[CHUNK:], jnp.where(head0_kv, zero, vw), preferred_element_type=F32))

    def projection_pieces(k):
        rows = pl.ds(pl.multiple_of(k * CHUNK, CHUNK), CHUNK)

        def piece(c):
            cols = slice(c * SB_FILL_COLS, (c + 1) * SB_FILL_COLS)
            ro_ref[0, rows, cols] = jnp.dot(yr_ref[0, rows, :], wr_ref[:, cols], preferred_element_type=F32)
        return [functools.partial(piece, c) for c in range(D_MODEL // SB_FILL_COLS)]

    def windows(qs, kw, vw, valid_lo, valid_hi, r, fillers=()):
        n = len(qs)
        z, st, later, rs, outs = {}, {}, {}, [None] * n, [None] * n
        for t in range(n + 3):
            if t < n:
                z[t] = scores(qs[t], kw[t])
            if 0 <= t - 1 < n:
                st[t - 1] = log_sticks(z.pop(t - 1), valid_lo, valid_hi)
            if 0 <= t - 2 < n:
                later[t - 2] = later_sums(st[t - 2][2])
            if 0 <= t - 3 < n:
                u = t - 3
                a, rs[u] = weights(st[u][0], st[u][1], later.pop(u), valid_lo, valid_hi, r[u])
                outs[u] = attend(a, vw[u])
            if t < len(fillers):
                fillers[t]()
        assert len(fillers) <= n + 3
        return rs, outs

    def diagonal_windows(blocks, first, fillers=()):
        row0 = [pl.multiple_of(i * CHUNK, CHUNK) for i in blocks]
        qs = [stacked_q(r0) for r0 in row0]
        if first:
            kw = [jnp.concatenate([meta_k(), k_rows(0, CHUNK)], axis=0)]
            vw = [jnp.concatenate([meta_v(), v_rows(0, CHUNK)], axis=0)]
            valid_lo = s_loc >= PAD
        else:
            start = [pl.multiple_of(r0 - CHUNK, CHUNK) for r0 in row0]
            kw = [k_rows(s, SB_WIN) for s in start]
            vw = [v_rows(s, SB_WIN) for s in start]
            valid_lo = None
        r, out = windows(qs, kw, vw, valid_lo, causal, [None] * len(blocks), fillers)
        return qs, r, out

    def load_block(rows, meta, j):
        start = pl.multiple_of(jnp.maximum(j, 0) * CHUNK, CHUNK)
        return jnp.where(j < 0, meta(), rows(start, CHUNK))

    def earlier_windows(i, qs, r, acc):
        def window(state):
            j_hi, r, acc = state
            j_lo = j_hi - 1
            kw = jnp.concatenate([load_block(k_rows, meta_k, j_lo), load_block(k_rows, meta_k, j_hi)], axis=0)
            vw = jnp.concatenate([load_block(v_rows, meta_v, j_lo), load_block(v_rows, meta_v, j_hi)], axis=0)
            valid_lo = j_lo * CHUNK + s_loc >= -N_META
            valid_hi = j_hi * CHUNK + s_loc >= -N_META
            r, out = windows([qs], [kw], [vw], valid_lo, valid_hi, [r])
            return j_hi - 2, r[0], acc + out[0]

        def more(state):
            j_hi, r, _ = state
            return (j_hi >= -1) & (jnp.min(r) < SB_STOP)

        return lax.while_loop(more, window, (i - 2, r, acc))[2]

    _, _, out = diagonal_windows([0], True, projection_pieces(0))
    o_ref[0, 0:CHUNK, :] = out[0].astype(BF16)

    def store(i, acc):
        o_ref[0, pl.ds(pl.multiple_of(i * CHUNK, CHUNK), CHUNK), :] = acc.astype(BF16)

    def finish_group(g):
        def block(u, carry):
            r = r_ref[g & 1, u]

            @pl.when(jnp.min(r) < SB_STOP)
            def _():
                i = 1 + g * SB_GROUP + u
                row0 = pl.multiple_of(i * CHUNK, CHUNK)
                acc = o_ref[0, pl.ds(row0, CHUNK), :].astype(F32)
                store(i, earlier_windows(i, stacked_q(row0), r, acc))
            return carry
        lax.fori_loop(0, SB_GROUP, block, 0)

    def group(g, r_prev):
        prev_unfinished = jnp.min(r_prev) < SB_STOP
        blocks = [1 + g * SB_GROUP + u for u in range(SB_GROUP)]
        _, r, out = diagonal_windows(blocks, False, projection_pieces(g + 1))
        for u, i in enumerate(blocks):
            store(i, out[u])
            r_ref[g & 1, u] = r[u]

        @pl.when(prev_unfinished)
        def _():
            finish_group(g - 1)

        return functools.reduce(jnp.minimum, r)

    assert (nq - 1) % SB_GROUP == 0
    n_groups = (nq - 1) // SB_GROUP
    assert (n_groups + 1) * CHUNK == yr_ref.shape[1]
    r_last = lax.fori_loop(0, n_groups, group, jnp.full((2 * CHUNK, 1), SB_STOP, F32))

    @pl.when(jnp.min(r_last) < SB_STOP)
    def _():
        finish_group(n_groups - 1)


def _sb_tables():
    j = np.arange(SB_WIN)[:, None]
    s = np.arange(SB_WIN)[None, :]
    return jnp.asarray((j > s).astype(np.float32), BF16)


def _stickbreak(proj, mproj, yret, wr, seq):
    batch = proj.shape[0]
    lanes = 2 * SB_DH
    pairs = SB_HEADS // 2
    slab = seq // pairs
    return pl.pallas_call(
        _sb_kernel,
        grid=(batch, pairs),
        in_specs=[
            pl.BlockSpec((1, seq, lanes), lambda b, p: (b, 0, (OFF_SQ - RET_COLS) // lanes + p)),
            pl.BlockSpec((1, seq, lanes), lambda b, p: (b, 0, (OFF_SK - RET_COLS) // lanes + p)),
            pl.BlockSpec((1, seq, lanes), lambda b, p: (b, 0, (OFF_SV - RET_COLS) // lanes + p)),
            pl.BlockSpec((CHUNK, lanes), lambda b, p: (0, (OFF_SK - RET_COLS) // lanes + p)),
            pl.BlockSpec((CHUNK, lanes), lambda b, p: (0, (OFF_SV - RET_COLS) // lanes + p)),
            pl.BlockSpec((SB_WIN, SB_WIN), lambda b, p: (0, 0)),
            pl.BlockSpec((1, slab, RET_HEADS * RET_DV), lambda b, p: (b, p, 0)),
            pl.BlockSpec(wr.shape, lambda b, p: (0, 0), pipeline_mode=pl.Buffered(1)),
        ],
        out_specs=[
            pl.BlockSpec((1, seq, lanes), lambda b, p: (b, 0, p)),
            pl.BlockSpec((1, slab, D_MODEL), lambda b, p: (b, p, 0)),
        ],
        out_shape=[
            jax.ShapeDtypeStruct((batch, seq, SB_HEADS * SB_DH), BF16),
            jax.ShapeDtypeStruct((batch, seq, D_MODEL), F32),
        ],
        scratch_shapes=[pltpu.VMEM((2, SB_GROUP, 2 * CHUNK, 1), F32)],
        compiler_params=pltpu.CompilerParams(
            dimension_semantics=("parallel", "parallel"),
            vmem_limit_bytes=VMEM_LIMIT),
        name="stickbreak",
    )(proj, proj, proj, mproj, mproj, _sb_tables(), yret, wr)


FFN_COLS = 256
POST_PARTS = 2


def _post_kernel(r_ref, ys_ref, ga_ref, gb_ref, x_ref, ws_ref, wo_ref, gmix_ref,
                 gpre_ref, wi_ref, wf_ref, gpost_ref, o_ref, u_ref):
    tm = o_ref.shape[0]
    parts = [slice(p * tm // POST_PARTS, (p + 1) * tm // POST_PARTS) for p in range(POST_PARTS)]
    dot = functools.partial(jnp.dot, preferred_element_type=F32)

    r = [r_ref[p, :] for p in parts]
    s = [dot(ys_ref[p, :], ws_ref[...]) for p in parts]
    mix = []
    for i, p in enumerate(parts):
        merged = (jax.nn.sigmoid(ga_ref[p, :].astype(F32)) * r[i]
                  + jax.nn.sigmoid(gb_ref[p, :].astype(F32)) * s[i])
        mix.append(dot(merged.astype(BF16), wo_ref[...]))
    h, hn = [], []
    for i, p in enumerate(parts):
        h.append(x_ref[p, :] + _rms(mix[i], gmix_ref[...]))
        hn.append(_rms(h[i], gpre_ref[...]).astype(BF16))
    for c in range(D_FF // FFN_COLS):
        lo = c * FFN_COLS
        for i, p in enumerate(parts):
            a = dot(hn[i], wi_ref[:, lo:lo + FFN_COLS])
            b = dot(hn[i], wi_ref[:, D_FF + lo:D_FF + lo + FFN_COLS])
            u_ref[p, lo:lo + FFN_COLS] = (a * jax.nn.sigmoid(a) * b).astype(BF16)
    ff = [dot(u_ref[p, :], wf_ref[...]) for p in parts]
    for i, p in enumerate(parts):
        o_ref[p, :] = h[i] + _rms(ff[i], gpost_ref[...])


def _post(r, ysb, proj2d, x2d, ws, wo, gmix, gpre, wi, wf, gpost, tm=512):
    rows = x2d.shape[0]
    row_block = lambda width, col=0: pl.BlockSpec((tm, width), lambda i: (i, col))
    return pl.pallas_call(
        _post_kernel,
        grid=(rows // tm,),
        in_specs=[
            row_block(D_MODEL),
            row_block(D_MODEL),
            row_block(D_MODEL, (OFF_GA - RET_COLS) // D_MODEL),
            row_block(D_MODEL, (OFF_GB - RET_COLS) // D_MODEL),
            row_block(D_MODEL),
            _resident(ws.shape), _resident(wo.shape),
            _resident((1, D_MODEL)), _resident((1, D_MODEL)),
            _resident(wi.shape), _resident(wf.shape),
            _resident((1, D_MODEL)),
        ],
        out_specs=row_block(D_MODEL),
        out_shape=jax.ShapeDtypeStruct((rows, D_MODEL), F32),
        scratch_shapes=[pltpu.VMEM((tm, D_FF), BF16)],
        compiler_params=pltpu.CompilerParams(
            dimension_semantics=("parallel",),
            vmem_limit_bytes=VMEM_LIMIT),
        name="post",
    )(r, ysb, proj2d, proj2d, x2d, ws, wo, gmix, gpre, wi, wf, gpost)


def kernel(x, meta_tokens, w_in, w_ret_out, w_sb_out, w_out, w_ffn_in, w_ffn_out,
           norm_mix_pre, norm_mix_post, norm_ffn_pre, norm_ffn_post):
    batch, seq, d = x.shape
    assert d == D_MODEL and seq % 1024 == 0 and w_in.shape[0] == 1
    x2d = x.reshape(batch * seq, d)
    meta_chunk = jnp.concatenate(
        [jnp.zeros((PAD, d), x.dtype), meta_tokens.astype(x.dtype)], axis=0)

    proj2d, mproj, yret, wr, ws, wo, wi, wf = _front(
        x2d, meta_chunk, norm_mix_pre, w_in[0].astype(BF16), seq,
        [w_ret_out, w_sb_out, w_out, w_ffn_in, w_ffn_out])
    ysb, r = _stickbreak(proj2d.reshape(batch, seq, OUT_COLS), mproj,
                         yret.reshape(batch, seq, -1), wr, seq)

    out = _post(r.reshape(batch * seq, -1), ysb.reshape(batch * seq, -1), proj2d, x2d, ws, wo,
                norm_mix_post, norm_ffn_pre, wi, wf, norm_ffn_post)
    return out.reshape(batch, seq, d)
```

```python
import functools

import jax
import jax.numpy as jnp
import numpy as np
from jax import lax
from jax.experimental import pallas as pl
from jax.experimental.pallas import tpu as pltpu

D_MODEL = 1024
N_META = 16
CHUNK = 128
PAD = CHUNK - N_META
RET_HEADS = 4
RET_DK = 256
RET_DV = 512
SB_HEADS = 16
SB_DH = 64
D_FF = 2816
ROPE_BASE = 10000.0
NORM_EPS = 1e-6
GN_EPS = 1e-5

OFF_RQ = 0
OFF_RK = 1024
OFF_RV = 2048
OFF_RG = 4096
OFF_SQ = 6144
OFF_SK = 7168
OFF_SV = 8192
OFF_GA = 9216
OFF_GB = 10240
PROJ_WIDTH = 11264

F32 = jnp.float32
BF16 = jnp.bfloat16

LOG2E = 1.4426950408889634
SB_STOP = 127.0
SB_WIN = 2 * CHUNK

VMEM_LIMIT = 60 * 1024 * 1024


def _rms(xf, g):
    ms = jnp.mean(xf * xf, axis=-1, keepdims=True)
    return xf * lax.rsqrt(ms + NORM_EPS) * g


def _resident(shape):
    return pl.BlockSpec(shape, lambda i: (0,) * len(shape), pipeline_mode=pl.Buffered(1))


FRONT_ROWS = 256
FRONT_COLS = 1024
W_CHUNK = 512
RET_COLS = OFF_SQ
OUT_COLS = PROJ_WIDTH - RET_COLS


def _rotate(x, cos, sin):
    half = RET_DK // 2
    x1, x2 = x[:, :half], x[:, half:]
    return jnp.concatenate([x1 * cos - x2 * sin, x1 * sin + x2 * cos], axis=1)


def _kv_state(k_rot, v, zeta):
    kz = (k_rot * zeta).astype(BF16)
    return lax.dot_general(kz, v, (((0,), (0,)), ((), ())), preferred_element_type=F32)


def _front_kernel(tiles_per_batch, n_cast,
                  x_ref, m_ref, g_ref, w_ref, cbase_ref, sbase_ref, coff_ref, soff_ref,
                  mcos_ref, msin_ref, mzeta_ref, decay_ref, xi_ref, zeta_ref, gchunk_ref, *rest):
    cast_in, (o_ref, mo_ref, y_ref), rest = rest[:n_cast], rest[n_cast:n_cast + 3], rest[n_cast + 3:]
    cast_out, (ring_ref, state_ref, mret_ref, wb_ref, stage_ref, sem) = rest[:n_cast], rest[n_cast:]
    j = pl.program_id(0)

    for src, dst in zip(cast_in, cast_out):
        dst[...] = src[0].astype(BF16)
    cur = j % 2
    prev = 1 - cur
    n_col = PROJ_WIDTH // FRONT_COLS
    n_ret = RET_COLS // FRONT_COLS

    def normed(rows):
        return _rms(rows, g_ref[...]).astype(BF16)

    def project(hn, put):
        for t in range(n_col):
            cols = slice(t * FRONT_COLS, (t + 1) * FRONT_COLS)
            put(t, jnp.dot(hn, wb_ref[:, cols], preferred_element_type=F32).astype(BF16))
            yield t

    @pl.when(j == 0)
    def _():
        def chunk(c, slot):
            cols = pl.ds(c * W_CHUNK, W_CHUNK)
            return pltpu.make_async_copy(w_ref.at[0, :, cols], stage_ref.at[slot], sem.at[slot])
        n_chunks = PROJ_WIDTH // W_CHUNK
        chunk(0, 0).start()
        for c in range(n_chunks):
            slot = c % 2
            if c + 1 < n_chunks:
                chunk(c + 1, 1 - slot).start()
            chunk(c, slot).wait()
            wb_ref[:, c * W_CHUNK:(c + 1) * W_CHUNK] = stage_ref[slot].astype(BF16)

        ring_ref[1] = jnp.zeros(ring_ref.shape[1:], BF16)
        state_ref[...] = jnp.zeros(state_ref.shape, F32)

        def put_meta(t, res):
            lo = t * FRONT_COLS
            if OFF_RK <= lo < OFF_RG:
                mret_ref[:, lo - OFF_RK:lo - OFF_RK + FRONT_COLS] = res
            elif lo >= RET_COLS:
                mo_ref[:, lo - RET_COLS:lo - RET_COLS + FRONT_COLS] = res
        for _ in project(normed(m_ref[...]), put_meta):
            pass

    @pl.when((j - 1) % tiles_per_batch == 0)
    def _():
        mcos, msin = mcos_ref[...], msin_ref[...]
        for h in range(RET_HEADS):
            k = mret_ref[:, h * RET_DK:(h + 1) * RET_DK].astype(F32)
            v = mret_ref[:, OFF_RV - OFF_RK + h * RET_DV:OFF_RV - OFF_RK + (h + 1) * RET_DV]
            state_ref[h] = _kv_state(_rotate(k, mcos, msin), v, mzeta_ref[h])

    co, so = coff_ref[...], soff_ref[...]
    state = [state_ref[h] for h in range(RET_HEADS)]
    trig = {}
    held = {}

    def phase1(h):
        if not trig:
            cb, sb = cbase_ref[0], sbase_ref[0]
            trig[0] = (cb * co - sb * so, sb * co + cb * so)
        cos, sin = trig[0]
        q = ring_ref[prev, :, OFF_RQ + h * RET_DK:OFF_RQ + (h + 1) * RET_DK].astype(F32)
        k = ring_ref[prev, :, OFF_RK + h * RET_DK:OFF_RK + (h + 1) * RET_DK].astype(F32)
        v = ring_ref[prev, :, OFF_RV + h * RET_DV:OFF_RV + (h + 1) * RET_DV]
        q_rot = (_rotate(q, cos, sin) * (RET_DK ** -0.5)).astype(BF16)
        k_rot = _rotate(k, cos, sin)
        scores = lax.dot_general(q_rot, k_rot.astype(BF16), (((1,), (1,)), ((), ())),
                                 preferred_element_type=F32)
        held[h] = ((scores * decay_ref[h]).astype(BF16), q_rot, (k_rot * zeta_ref[h]).astype(BF16), v)

    def phase2(h):
        sd, q_rot, kz, v = held.pop(h)
        inner = jnp.dot(sd, v, preferred_element_type=F32)
        cross = jnp.dot(q_rot, state[h].astype(BF16), preferred_element_type=F32)
        kv = lax.dot_general(kz, v, (((0,), (0,)), ((), ())), preferred_element_type=F32)
        state[h] = gchunk_ref[h, 0:1, :] * state[h] + kv
        y = inner + cross * xi_ref[h]
        mu = jnp.mean(y, axis=-1, keepdims=True)
        yc = y - mu
        var = jnp.mean(yc * yc, axis=-1, keepdims=True)
        yn = yc * lax.rsqrt(var + GN_EPS)
        gate = ring_ref[prev, :, OFF_RG + h * RET_DV:OFF_RG + (h + 1) * RET_DV].astype(F32)
        y_ref[:, h * RET_DV:(h + 1) * RET_DV] = (gate * jax.nn.sigmoid(gate) * yn).astype(BF16)

    def put(t, res):
        lo = t * FRONT_COLS
        if t < n_ret:
            ring_ref[cur, :, lo:lo + FRONT_COLS] = res
        else:
            o_ref[:, lo - RET_COLS:lo - RET_COLS + FRONT_COLS] = res

    for t in project(normed(x_ref[...]), put):
        if t < 2 * RET_HEADS:
            (phase1 if t % 2 == 0 else phase2)(t // 2)
    assert 2 * RET_HEADS <= n_col

    for h in range(RET_HEADS):
        state_ref[h] = state[h]


def _retention_tables(chunk):
    hh = np.arange(RET_HEADS, dtype=np.float64)
    log_g = np.log1p(-(2.0 ** (-5.0 - hh)))
    idx = np.arange(chunk, dtype=np.float64)
    diff = idx[:, None] - idx[None, :]
    decay = np.where(diff >= 0, np.exp(log_g[:, None, None] * np.maximum(diff, 0.0)), 0.0)
    zeta = np.exp(log_g[:, None] * (chunk - 1.0 - idx))[:, :, None]
    xi = np.exp(log_g[:, None] * (idx + 1.0))[:, :, None]
    gchunk = np.exp(log_g * chunk)[:, None, None]
    bc = lambda a, shape: jnp.asarray(np.broadcast_to(a, shape), F32)
    return (bc(decay, (RET_HEADS, chunk, chunk)),
            bc(xi, (RET_HEADS, chunk, RET_DV)),
            bc(zeta, (RET_HEADS, chunk, RET_DK)),
            bc(gchunk, (RET_HEADS, 8, RET_DV)))


def _rope_tables(seq, chunk):
    half = RET_DK // 2
    inv = ROPE_BASE ** (-np.arange(half, dtype=np.float64) / half)
    base = (N_META + chunk * np.arange(seq // chunk, dtype=np.float64))[:, None, None] * inv
    off = np.arange(chunk, dtype=np.float64)[:, None] * inv
    meta = (np.arange(CHUNK, dtype=np.float64) - PAD)[:, None] * inv
    f = lambda a: jnp.asarray(a, F32)
    return (f(np.cos(base)), f(np.sin(base)), f(np.cos(off)), f(np.sin(off)),
            f(np.cos(meta)), f(np.sin(meta)))


def _cast_slab(rows, steps):
    slab = next(s for s in range(16, rows + 1, 16) if rows % s == 0 and rows // s <= steps)
    return slab, rows // slab


def _front(x2d, meta_chunk, gain, w_in, seq, to_cast):
    rows = x2d.shape[0]
    tm = FRONT_ROWS
    n_tiles = rows // tm
    tiles_per_batch = seq // tm
    cbase, sbase, coff, soff, mcos, msin = _rope_tables(seq, tm)
    decay, xi, zeta, gchunk = _retention_tables(tm)
    mzeta = _retention_tables(CHUNK)[2]
    this_tile = lambda j: (jnp.minimum(j, n_tiles - 1), 0)
    last_tile = lambda j: (jnp.maximum(j - 1, 0), 0)
    last_base = lambda j: (jnp.maximum(j - 1, 0) % tiles_per_batch, 0, 0)
    cast_in_specs, cast_out_specs = [], []
    for w in to_cast:
        slab, blocks = _cast_slab(w.shape[1], n_tiles)
        cast_in_specs.append(pl.BlockSpec(
            (1, slab, w.shape[2]), lambda j, blocks=blocks: (0, jnp.minimum(j, blocks - 1), 0)))
        cast_out_specs.append(pl.BlockSpec(
            (slab, w.shape[2]), lambda j, blocks=blocks: (jnp.minimum(j, blocks - 1), 0)))
    return pl.pallas_call(
        functools.partial(_front_kernel, tiles_per_batch, len(to_cast)),
        grid=(n_tiles + 1,),
        in_specs=[
            pl.BlockSpec((tm, D_MODEL), this_tile),
            _resident((CHUNK, D_MODEL)),
            _resident((1, D_MODEL)),
            pl.BlockSpec(memory_space=pl.ANY),
            pl.BlockSpec((1, 1, RET_DK // 2), last_base),
            pl.BlockSpec((1, 1, RET_DK // 2), last_base),
            _resident((tm, RET_DK // 2)), _resident((tm, RET_DK // 2)),
            _resident((CHUNK, RET_DK // 2)), _resident((CHUNK, RET_DK // 2)),
            _resident((RET_HEADS, CHUNK, RET_DK)),
            _resident((RET_HEADS, tm, tm)),
            _resident((RET_HEADS, tm, RET_DV)),
            _resident((RET_HEADS, tm, RET_DK)),
            _resident((RET_HEADS, 8, RET_DV)),
        ] + cast_in_specs,
        out_specs=[
            pl.BlockSpec((tm, OUT_COLS), this_tile),
            pl.BlockSpec((CHUNK, OUT_COLS), lambda j: (0, 0)),
            pl.BlockSpec((tm, RET_HEADS * RET_DV), last_tile),
        ] + cast_out_specs,
        out_shape=[
            jax.ShapeDtypeStruct((rows, OUT_COLS), BF16),
            jax.ShapeDtypeStruct((CHUNK, OUT_COLS), BF16),
            jax.ShapeDtypeStruct((rows, RET_HEADS * RET_DV), BF16),
        ] + [jax.ShapeDtypeStruct(w.shape[1:], BF16) for w in to_cast],
        scratch_shapes=[
            pltpu.VMEM((2, tm, RET_COLS), BF16),
            pltpu.VMEM((RET_HEADS, RET_DK, RET_DV), F32),
            pltpu.VMEM((CHUNK, OFF_RG - OFF_RK), BF16),
            pltpu.VMEM((D_MODEL, PROJ_WIDTH), BF16),
            pltpu.VMEM((2, D_MODEL, W_CHUNK), F32),
            pltpu.SemaphoreType.DMA((2,)),
        ],
        compiler_params=pltpu.CompilerParams(
            dimension_semantics=("arbitrary",),
            vmem_limit_bytes=VMEM_LIMIT),
        name="front",
    )(x2d, meta_chunk, gain, w_in, cbase, sbase, coff, soff, mcos, msin, mzeta, decay, xi, zeta, gchunk,
      *to_cast)


SB_GROUP = 9
SB_FILL_COLS = 256


def _softplus2(z):
    return jnp.maximum(z, jnp.log2(1.0 + jnp.exp2(jnp.minimum(z, 126.0))))


def _sb_kernel(q_ref, k_ref, v_ref, mk_ref, mv_ref, tri_ref, yr_ref, wr_ref, o_ref, ro_ref, r_ref):
    nq = q_ref.shape[1] // CHUNK
    lanes = 2 * SB_DH
    q_rows = lambda start, size: q_ref[0, pl.ds(start, size), :]
    k_rows = lambda start, size: k_ref[0, pl.ds(start, size), :]
    v_rows = lambda start, size: v_ref[0, pl.ds(start, size), :]
    meta_k = lambda: mk_ref[...]
    meta_v = lambda: mv_ref[...]
    head0_q = lax.broadcasted_iota(jnp.int32, (CHUNK, lanes), 1) < SB_DH
    head0_kv = lax.broadcasted_iota(jnp.int32, (SB_WIN, lanes), 1) < SB_DH
    t_loc = lax.broadcasted_iota(jnp.int32, (2 * CHUNK, CHUNK), 0) & (CHUNK - 1)
    s_loc = lax.broadcasted_iota(jnp.int32, (2 * CHUNK, CHUNK), 1)
    causal = s_loc < t_loc
    tri = tri_ref[...]

    def stacked_q(row0):
        qf = q_rows(row0, CHUNK).astype(F32) * (SB_DH ** -0.5 * LOG2E)
        both = jnp.concatenate([jnp.where(head0_q, qf, 0.0), jnp.where(head0_q, 0.0, qf)], axis=0)
        return both.astype(BF16)

    mask = lambda x, m: x if m is None else jnp.where(m, x, 0.0)

    def scores(qs, kw):
        return lax.dot_general(qs, kw, (((1,), (1,)), ((), ())), preferred_element_type=F32)

    def log_sticks(z, valid_lo, valid_hi):
        sp = _softplus2(z)
        ln = jnp.concatenate([mask(sp[:, :CHUNK], valid_lo), mask(sp[:, CHUNK:], valid_hi)], axis=1)
        return z - sp, ln[:, 0:1], ln.astype(BF16)

    def later_sums(ln):
        return jnp.dot(ln, tri, preferred_element_type=F32)

    def weights(log_beta, ln0, later, valid_lo, valid_hi, r):
        total = later[:, 0:1] + ln0
        if r is not None:
            later = later + r
            total = total + r
        a = jnp.exp2(log_beta - later)
        a = jnp.concatenate([mask(a[:, :CHUNK], valid_lo), mask(a[:, CHUNK:], valid_hi)], axis=1)
        return a.astype(BF16), total

    def attend(a, vw):
        zero = jnp.zeros_like(vw)
        return (jnp.dot(a[:CHUNK], jnp.where(head0_kv, vw, zero), preferred_element_type=F32)
                + jnp.dot(a[CHUNK:], jnp.where(head0_kv, zero, vw), preferred_element_type=F32))

    def projection_pieces(k):
        rows = pl.ds(pl.multiple_of(k * CHUNK, CHUNK), CHUNK)

        def piece(c):
            cols = slice(c * SB_FILL_COLS, (c + 1) * SB_FILL_COLS)
            ro_ref[0, rows, cols] = jnp.dot(yr_ref[0, rows, :], wr_ref[:, cols], preferred_element_type=F32)
        return [functools.partial(piece, c) for c in range(D_MODEL // SB_FILL_COLS)]

    def windows(qs, kw, vw, valid_lo, valid_hi, r, fillers=()):
        n = len(qs)
        z, st, later, rs, outs = {}, {}, {}, [None] * n, [None] * n
        for t in range(n + 3):
            if t < n:
                z[t] = scores(qs[t], kw[t])
            if 0 <= t - 1 < n:
                st[t - 1] = log_sticks(z.pop(t - 1), valid_lo, valid_hi)
            if 0 <= t - 2 < n:
                later[t - 2] = later_sums(st[t - 2][2])
            if 0 <= t - 3 < n:
                u = t - 3
                a, rs[u] = weights(st[u][0], st[u][1], later.pop(u), valid_lo, valid_hi, r[u])
                outs[u] = attend(a, vw[u])
            if t < len(fillers):
                fillers[t]()
        assert len(fillers) <= n + 3
        return rs, outs

    def diagonal_windows(blocks, first, fillers=()):
        row0 = [pl.multiple_of(i * CHUNK, CHUNK) for i in blocks]
        qs = [stacked_q(r0) for r0 in row0]
        if first:
            kw = [jnp.concatenate([meta_k(), k_rows(0, CHUNK)], axis=0)]
            vw = [jnp.concatenate([meta_v(), v_rows(0, CHUNK)], axis=0)]
            valid_lo = s_loc >= PAD
        else:
            start = [pl.multiple_of(r0 - CHUNK, CHUNK) for r0 in row0]
            kw = [k_rows(s, SB_WIN) for s in start]
            vw = [v_rows(s, SB_WIN) for s in start]
            valid_lo = None
        r, out = windows(qs, kw, vw, valid_lo, causal, [None] * len(blocks), fillers)
        return qs, r, out

    def load_block(rows, meta, j):
        start = pl.multiple_of(jnp.maximum(j, 0) * CHUNK, CHUNK)
        return jnp.where(j < 0, meta(), rows(start, CHUNK))

    def earlier_windows(i, qs, r, acc):
        def window(state):
            j_hi, r, acc = state
            j_lo = j_hi - 1
            kw = jnp.concatenate([load_block(k_rows, meta_k, j_lo), load_block(k_rows, meta_k, j_hi)], axis=0)
            vw = jnp.concatenate([load_block(v_rows, meta_v, j_lo), load_block(v_rows, meta_v, j_hi)], axis=0)
            valid_lo = j_lo * CHUNK + s_loc >= -N_META
            valid_hi = j_hi * CHUNK + s_loc >= -N_META
            r, out = windows([qs], [kw], [vw], valid_lo, valid_hi, [r])
            return j_hi - 2, r[0], acc + out[0]

        def more(state):
            j_hi, r, _ = state
            return (j_hi >= -1) & (jnp.min(r) < SB_STOP)

        return lax.while_loop(more, window, (i - 2, r, acc))[2]

    _, _, out = diagonal_windows([0], True, projection_pieces(0))
    o_ref[0, 0:CHUNK, :] = out[0].astype(BF16)

    def store(i, acc):
        o_ref[0, pl.ds(pl.multiple_of(i * CHUNK, CHUNK), CHUNK), :] = acc.astype(BF16)

    def finish_group(g):
        def block(u, carry):
            r = r_ref[g & 1, u]

            @pl.when(jnp.min(r) < SB_STOP)
            def _():
                i = 1 + g * SB_GROUP + u
                row0 = pl.multiple_of(i * CHUNK, CHUNK)
                acc = o_ref[0, pl.ds(row0, CHUNK), :].astype(F32)
                store(i, earlier_windows(i, stacked_q(row0), r, acc))
            return carry
        lax.fori_loop(0, SB_GROUP, block, 0)

    def group(g, r_prev):
        prev_unfinished = jnp.min(r_prev) < SB_STOP
        blocks = [1 + g * SB_GROUP + u for u in range(SB_GROUP)]
        _, r, out = diagonal_windows(blocks, False, projection_pieces(g + 1))
        for u, i in enumerate(blocks):
            store(i, out[u])
            r_ref[g & 1, u] = r[u]

        @pl.when(prev_unfinished)
        def _():
            finish_group(g - 1)

        return functools.reduce(jnp.minimum, r)

    assert (nq - 1) % SB_GROUP == 0
    n_groups = (nq - 1) // SB_GROUP
    assert (n_groups + 1) * CHUNK == yr_ref.shape[1]
    r_last = lax.fori_loop(0, n_groups, group, jnp.full((2 * CHUNK, 1), SB_STOP, F32))

    @pl.when(jnp.min(r_last) < SB_STOP)
    def _():
        finish_group(n_groups - 1)


def _sb_tables():
    j = np.arange(SB_WIN)[:, None]
    s = np.arange(SB_WIN)[None, :]
    return jnp.asarray((j > s).astype(np.float32), BF16)


def _stickbreak(proj, mproj, yret, wr, seq):
    batch = proj.shape[0]
    lanes = 2 * SB_DH
    pairs = SB_HEADS // 2
    slab = seq // pairs
    return pl.pallas_call(
        _sb_kernel,
        grid=(batch, pairs),
        in_specs=[
            pl.BlockSpec((1, seq, lanes), lambda b, p: (b, 0, (OFF_SQ - RET_COLS) // lanes + p)),
            pl.BlockSpec((1, seq, lanes), lambda b, p: (b, 0, (OFF_SK - RET_COLS) // lanes + p)),
            pl.BlockSpec((1, seq, lanes), lambda b, p: (b, 0, (OFF_SV - RET_COLS) // lanes + p)),
            pl.BlockSpec((CHUNK, lanes), lambda b, p: (0, (OFF_SK - RET_COLS) // lanes + p)),
            pl.BlockSpec((CHUNK, lanes), lambda b, p: (0, (OFF_SV - RET_COLS) // lanes + p)),
            pl.BlockSpec((SB_WIN, SB_WIN), lambda b, p: (0, 0)),
            pl.BlockSpec((1, slab, RET_HEADS * RET_DV), lambda b, p: (b, p, 0)),
            pl.BlockSpec(wr.shape, lambda b, p: (0, 0), pipeline_mode=pl.Buffered(1)),
        ],
        out_specs=[
            pl.BlockSpec((1, seq, lanes), lambda b, p: (b, 0, p)),
            pl.BlockSpec((1, slab, D_MODEL), lambda b, p: (b, p, 0)),
        ],
        out_shape=[
            jax.ShapeDtypeStruct((batch, seq, SB_HEADS * SB_DH), BF16),
            jax.ShapeDtypeStruct((batch, seq, D_MODEL), F32),
        ],
        scratch_shapes=[pltpu.VMEM((2, SB_GROUP, 2 * CHUNK, 1), F32)],
        compiler_params=pltpu.CompilerParams(
            dimension_semantics=("parallel", "parallel"),
            vmem_limit_bytes=VMEM_LIMIT),
        name="stickbreak",
    )(proj, proj, proj, mproj, mproj, _sb_tables(), yret, wr)


FFN_COLS = 256
POST_PARTS = 2


def _post_kernel(r_ref, ys_ref, ga_ref, gb_ref, x_ref, ws_ref, wo_ref, gmix_ref,
                 gpre_ref, wi_ref, wf_ref, gpost_ref, o_ref, u_ref):
    tm = o_ref.shape[0]
    parts = [slice(p * tm // POST_PARTS, (p + 1) * tm // POST_PARTS) for p in range(POST_PARTS)]
    dot = functools.partial(jnp.dot, preferred_element_type=F32)

    r = [r_ref[p, :] for p in parts]
    s = [dot(ys_ref[p, :], ws_ref[...]) for p in parts]
    mix = []
    for i, p in enumerate(parts):
        merged = (jax.nn.sigmoid(ga_ref[p, :].astype(F32)) * r[i]
                  + jax.nn.sigmoid(gb_ref[p, :].astype(F32)) * s[i])
        mix.append(dot(merged.astype(BF16), wo_ref[...]))
    h, hn = [], []
    for i, p in enumerate(parts):
        h.append(x_ref[p, :] + _rms(mix[i], gmix_ref[...]))
        hn.append(_rms(h[i], gpre_ref[...]).astype(BF16))
    for c in range(D_FF // FFN_COLS):
        lo = c * FFN_COLS
        for i, p in enumerate(parts):
            a = dot(hn[i], wi_ref[:, lo:lo + FFN_COLS])
            b = dot(hn[i], wi_ref[:, D_FF + lo:D_FF + lo + FFN_COLS])
            u_ref[p, lo:lo + FFN_COLS] = (a * jax.nn.sigmoid(a) * b).astype(BF16)
    ff = [dot(u_ref[p, :], wf_ref[...]) for p in parts]
    for i, p in enumerate(parts):
        o_ref[p, :] = h[i] + _rms(ff[i], gpost_ref[...])


def _post(r, ysb, proj2d, x2d, ws, wo, gmix, gpre, wi, wf, gpost, tm=512):
    rows = x2d.shape[0]
    row_block = lambda width, col=0: pl.BlockSpec((tm, width), lambda i: (i, col))
    return pl.pallas_call(
        _post_kernel,
        grid=(rows // tm,),
        in_specs=[
            row_block(D_MODEL),
            row_block(D_MODEL),
            row_block(D_MODEL, (OFF_GA - RET_COLS) // D_MODEL),
            row_block(D_MODEL, (OFF_GB - RET_COLS) // D_MODEL),
            row_block(D_MODEL),
            _resident(ws.shape), _resident(wo.shape),
            _resident((1, D_MODEL)), _resident((1, D_MODEL)),
            _resident(wi.shape), _resident(wf.shape),
            _resident((1, D_MODEL)),
        ],
        out_specs=row_block(D_MODEL),
        out_shape=jax.ShapeDtypeStruct((rows, D_MODEL), F32),
        scratch_shapes=[pltpu.VMEM((tm, D_FF), BF16)],
        compiler_params=pltpu.CompilerParams(
            dimension_semantics=("parallel",),
            vmem_limit_bytes=VMEM_LIMIT),
        name="post",
    )(r, ysb, proj2d, proj2d, x2d, ws, wo, gmix, gpre, wi, wf, gpost)


def kernel(x, meta_tokens, w_in, w_ret_out, w_sb_out, w_out, w_ffn_in, w_ffn_out,
           norm_mix_pre, norm_mix_post, norm_ffn_pre, norm_ffn_post):
    batch, seq, d = x.shape
    assert d == D_MODEL and seq % 1024 == 0 and w_in.shape[0] == 1
    x2d = x.reshape(batch * seq, d)
    meta_chunk = jnp.concatenate(
        [jnp.zeros((PAD, d), x.dtype), meta_tokens.astype(x.dtype)], axis=0)

    proj2d, mproj, yret, wr, ws, wo, wi, wf = _front(
        x2d, meta_chunk, norm_mix_pre, w_in, seq,
        [w_ret_out, w_sb_out, w_out, w_ffn_in, w_ffn_out])
    ysb, r = _stickbreak(proj2d.reshape(batch, seq, OUT_COLS), mproj,
                         yret.reshape(batch, seq, -1), wr, seq)

    out = _post(r.reshape(batch * seq, -1), ysb.reshape(batch * seq, -1), proj2d, x2d, ws, wo,
                norm_mix_post, norm_ffn_pre, wi, wf, norm_ffn_post)
    return out.reshape(batch, seq, d)
```

```python
import functools

import jax
import jax.numpy as jnp
import numpy as np
from jax import lax
from jax.experimental import pallas as pl
from jax.experimental.pallas import tpu as pltpu

D_MODEL = 1024
N_META = 16
CHUNK = 128
PAD = CHUNK - N_META
RET_HEADS = 4
RET_DK = 256
RET_DV = 512
SB_HEADS = 16
SB_DH = 64
D_FF = 2816
ROPE_BASE = 10000.0
NORM_EPS = 1e-6
GN_EPS = 1e-5

OFF_RQ = 0
OFF_RK = 1024
OFF_RV = 2048
OFF_RG = 4096
OFF_SQ = 6144
OFF_SK = 7168
OFF_SV = 8192
OFF_GA = 9216
OFF_GB = 10240
PROJ_WIDTH = 11264

F32 = jnp.float32
BF16 = jnp.bfloat16

LOG2E = 1.4426950408889634
SB_STOP = 127.0
SB_WIN = 2 * CHUNK

VMEM_LIMIT = 60 * 1024 * 1024


def _rms(xf, g):
    ms = jnp.mean(xf * xf, axis=-1, keepdims=True)
    return xf * lax.rsqrt(ms + NORM_EPS) * g


def _resident(shape):
    return pl.BlockSpec(shape, lambda i: (0,) * len(shape), pipeline_mode=pl.Buffered(1))


FRONT_ROWS = 256
FRONT_COLS = 1024
W_CHUNK = 512
RET_COLS = OFF_SQ
OUT_COLS = PROJ_WIDTH - RET_COLS


def _rotate(x, cos, sin):
    half = RET_DK // 2
    x1, x2 = x[:, :half], x[:, half:]
    return jnp.concatenate([x1 * cos - x2 * sin, x1 * sin + x2 * cos], axis=1)


def _kv_state(k_rot, v, zeta):
    kz = (k_rot * zeta).astype(BF16)
    return lax.dot_general(kz, v, (((0,), (0,)), ((), ())), preferred_element_type=F32)


def _front_kernel(tiles_per_batch, n_cast,
                  x_ref, m_ref, g_ref, w_ref, cbase_ref, sbase_ref, coff_ref, soff_ref,
                  mcos_ref, msin_ref, mzeta_ref, decay_ref, xi_ref, zeta_ref, gchunk_ref, *rest):
    cast_in, (o_ref, mo_ref, y_ref), rest = rest[:n_cast], rest[n_cast:n_cast + 3], rest[n_cast + 3:]
    cast_out, (ring_ref, state_ref, mret_ref, wb_ref, stage_ref, sem) = rest[:n_cast], rest[n_cast:]
    j = pl.program_id(0)

    for src, dst in zip(cast_in, cast_out):
        dst[...] = src[0].astype(BF16)
    cur = j % 2
    prev = 1 - cur
    n_col = PROJ_WIDTH // FRONT_COLS
    n_ret = RET_COLS // FRONT_COLS

    def normed(rows):
        return _rms(rows, g_ref[...]).astype(BF16)

    def project(hn, put):
        for t in range(n_col):
            cols = slice(t * FRONT_COLS, (t + 1) * FRONT_COLS)
            put(t, jnp.dot(hn, wb_ref[:, cols], preferred_element_type=F32).astype(BF16))
            yield t

    @pl.when(j == 0)
    def _():
        def chunk(c, slot):
            cols = pl.ds(c * W_CHUNK, W_CHUNK)
            return pltpu.make_async_copy(w_ref.at[0, :, cols], stage_ref.at[slot], sem.at[slot])
        n_chunks = PROJ_WIDTH // W_CHUNK
        chunk(0, 0).start()
        for c in range(n_chunks):
            slot = c % 2
            if c + 1 < n_chunks:
                chunk(c + 1, 1 - slot).start()
            chunk(c, slot).wait()
            wb_ref[:, c * W_CHUNK:(c + 1) * W_CHUNK] = stage_ref[slot].astype(BF16)

        ring_ref[1] = jnp.zeros(ring_ref.shape[1:], BF16)
        state_ref[...] = jnp.zeros(state_ref.shape, F32)

        def put_meta(t, res):
            lo = t * FRONT_COLS
            if OFF_RK <= lo < OFF_RG:
                mret_ref[:, lo - OFF_RK:lo - OFF_RK + FRONT_COLS] = res
            elif lo >= RET_COLS:
                mo_ref[:, lo - RET_COLS:lo - RET_COLS + FRONT_COLS] = res
        for _ in project(normed(m_ref[...]), put_meta):
            pass

    @pl.when((j - 1) % tiles_per_batch == 0)
    def _():
        mcos, msin = mcos_ref[...], msin_ref[...]
        for h in range(RET_HEADS):
            k = mret_ref[:, h * RET_DK:(h + 1) * RET_DK].astype(F32)
            v = mret_ref[:, OFF_RV - OFF_RK + h * RET_DV:OFF_RV - OFF_RK + (h + 1) * RET_DV]
            state_ref[h] = _kv_state(_rotate(k, mcos, msin), v, mzeta_ref[h])

    co, so = coff_ref[...], soff_ref[...]
    state = [state_ref[h] for h in range(RET_HEADS)]
    trig = {}
    held = {}

    def phase1(h):
        if not trig:
            cb, sb = cbase_ref[0], sbase_ref[0]
            trig[0] = (cb * co - sb * so, sb * co + cb * so)
        cos, sin = trig[0]
        q = ring_ref[prev, :, OFF_RQ + h * RET_DK:OFF_RQ + (h + 1) * RET_DK].astype(F32)
        k = ring_ref[prev, :, OFF_RK + h * RET_DK:OFF_RK + (h + 1) * RET_DK].astype(F32)
        v = ring_ref[prev, :, OFF_RV + h * RET_DV:OFF_RV + (h + 1) * RET_DV]
        q_rot = (_rotate(q, cos, sin) * (RET_DK ** -0.5)).astype(BF16)
        k_rot = _rotate(k, cos, sin)
        scores = lax.dot_general(q_rot, k_rot.astype(BF16), (((1,), (1,)), ((), ())),
                                 preferred_element_type=F32)
        held[h] = ((scores * decay_ref[h]).astype(BF16), q_rot, (k_rot * zeta_ref[h]).astype(BF16), v)

    def phase2(h):
        sd, q_rot, kz, v = held.pop(h)
        inner = jnp.dot(sd, v, preferred_element_type=F32)
        cross = jnp.dot(q_rot, state[h].astype(BF16), preferred_element_type=F32)
        kv = lax.dot_general(kz, v, (((0,), (0,)), ((), ())), preferred_element_type=F32)
        state[h] = gchunk_ref[h, 0:1, :] * state[h] + kv
        y = inner + cross * xi_ref[h]
        mu = jnp.mean(y, axis=-1, keepdims=True)
        yc = y - mu
        var = jnp.mean(yc * yc, axis=-1, keepdims=True)
        yn = yc * lax.rsqrt(var + GN_EPS)
        gate = ring_ref[prev, :, OFF_RG + h * RET_DV:OFF_RG + (h + 1) * RET_DV].astype(F32)
        y_ref[:, h * RET_DV:(h + 1) * RET_DV] = (gate * jax.nn.sigmoid(gate) * yn).astype(BF16)

    def put(t, res):
        lo = t * FRONT_COLS
        if t < n_ret:
            ring_ref[cur, :, lo:lo + FRONT_COLS] = res
        else:
            o_ref[:, lo - RET_COLS:lo - RET_COLS + FRONT_COLS] = res

    for t in project(normed(x_ref[...]), put):
        if t < 2 * RET_HEADS:
            (phase1 if t % 2 == 0 else phase2)(t // 2)
    assert 2 * RET_HEADS <= n_col

    for h in range(RET_HEADS):
        state_ref[h] = state[h]


def _retention_tables(chunk):
    hh = np.arange(RET_HEADS, dtype=np.float64)
    log_g = np.log1p(-(2.0 ** (-5.0 - hh)))
    idx = np.arange(chunk, dtype=np.float64)
    diff = idx[:, None] - idx[None, :]
    decay = np.where(diff >= 0, np.exp(log_g[:, None, None] * np.maximum(diff, 0.0)), 0.0)
    zeta = np.exp(log_g[:, None] * (chunk - 1.0 - idx))[:, :, None]
    xi = np.exp(log_g[:, None] * (idx + 1.0))[:, :, None]
    gchunk = np.exp(log_g * chunk)[:, None, None]
    bc = lambda a, shape: jnp.asarray(np.broadcast_to(a, shape), F32)
    return (bc(decay, (RET_HEADS, chunk, chunk)),
            bc(xi, (RET_HEADS, chunk, RET_DV)),
            bc(zeta, (RET_HEADS, chunk, RET_DK)),
            bc(gchunk, (RET_HEADS, 8, RET_DV)))


def _rope_tables(seq, chunk):
    half = RET_DK // 2
    inv = ROPE_BASE ** (-np.arange(half, dtype=np.float64) / half)
    base = (N_META + chunk * np.arange(seq // chunk, dtype=np.float64))[:, None, None] * inv
    off = np.arange(chunk, dtype=np.float64)[:, None] * inv
    meta = (np.arange(CHUNK, dtype=np.float64) - PAD)[:, None] * inv
    f = lambda a: jnp.asarray(a, F32)
    return (f(np.cos(base)), f(np.sin(base)), f(np.cos(off)), f(np.sin(off)),
            f(np.cos(meta)), f(np.sin(meta)))


def _cast_slab(rows, steps):
    slab = next(s for s in range(16, rows + 1, 16) if rows % s == 0 and rows // s <= steps)
    return slab, rows // slab


def _front(x2d, meta_chunk, gain, w_in, seq, to_cast):
    rows = x2d.shape[0]
    tm = FRONT_ROWS
    n_tiles = rows // tm
    tiles_per_batch = seq // tm
    cbase, sbase, coff, soff, mcos, msin = _rope_tables(seq, tm)
    decay, xi, zeta, gchunk = _retention_tables(tm)
    mzeta = _retention_tables(CHUNK)[2]
    this_tile = lambda j: (jnp.minimum(j, n_tiles - 1), 0)
    last_tile = lambda j: (jnp.maximum(j - 1, 0), 0)
    last_base = lambda j: (jnp.maximum(j - 1, 0) % tiles_per_batch, 0, 0)
    cast_in_specs, cast_out_specs = [], []
    for w in to_cast:
        slab, blocks = _cast_slab(w.shape[1], n_tiles)
        cast_in_specs.append(pl.BlockSpec(
            (1, slab, w.shape[2]), lambda j, blocks=blocks: (0, jnp.minimum(j, blocks - 1), 0)))
        cast_out_specs.append(pl.BlockSpec(
            (slab, w.shape[2]), lambda j, blocks=blocks: (jnp.minimum(j, blocks - 1), 0)))
    return pl.pallas_call(
        functools.partial(_front_kernel, tiles_per_batch, len(to_cast)),
        grid=(n_tiles + 1,),
        in_specs=[
            pl.BlockSpec((tm, D_MODEL), this_tile),
            _resident((CHUNK, D_MODEL)),
            _resident((1, D_MODEL)),
            pl.BlockSpec(memory_space=pl.ANY),
            pl.BlockSpec((1, 1, RET_DK // 2), last_base),
            pl.BlockSpec((1, 1, RET_DK // 2), last_base),
            _resident((tm, RET_DK // 2)), _resident((tm, RET_DK // 2)),
            _resident((CHUNK, RET_DK // 2)), _resident((CHUNK, RET_DK // 2)),
            _resident((RET_HEADS, CHUNK, RET_DK)),
            _resident((RET_HEADS, tm, tm)),
            _resident((RET_HEADS, tm, RET_DV)),
            _resident((RET_HEADS, tm, RET_DK)),
            _resident((RET_HEADS, 8, RET_DV)),
        ] + cast_in_specs,
        out_specs=[
            pl.BlockSpec((tm, OUT_COLS), this_tile),
            pl.BlockSpec((CHUNK, OUT_COLS), lambda j: (0, 0)),
            pl.BlockSpec((tm, RET_HEADS * RET_DV), last_tile),
        ] + cast_out_specs,
        out_shape=[
            jax.ShapeDtypeStruct((rows, OUT_COLS), BF16),
            jax.ShapeDtypeStruct((CHUNK, OUT_COLS), BF16),
            jax.ShapeDtypeStruct((rows, RET_HEADS * RET_DV), BF16),
        ] + [jax.ShapeDtypeStruct(w.shape[1:], BF16) for w in to_cast],
        scratch_shapes=[
            pltpu.VMEM((2, tm, RET_COLS), BF16),
            pltpu.VMEM((RET_HEADS, RET_DK, RET_DV), F32),
            pltpu.VMEM((CHUNK, OFF_RG - OFF_RK), BF16),
            pltpu.VMEM((D_MODEL, PROJ_WIDTH), BF16),
            pltpu.VMEM((2, D_MODEL, W_CHUNK), F32),
            pltpu.SemaphoreType.DMA((2,)),
        ],
        compiler_params=pltpu.CompilerParams(
            dimension_semantics=("arbitrary",),
            vmem_limit_bytes=VMEM_LIMIT),
        name="front",
    )(x2d, meta_chunk, gain, w_in, cbase, sbase, coff, soff, mcos, msin, mzeta, decay, xi, zeta, gchunk,
      *to_cast)


SB_GROUP = 9
SB_FILL_COLS = 256


def _softplus2(z):
    return jnp.maximum(z, jnp.log2(1.0 + jnp.exp2(jnp.minimum(z, 126.0))))


def _sb_kernel(q_ref, k_ref, v_ref, mk_ref, mv_ref, tri_ref, yr_ref, wr_ref, o_ref, ro_ref, r_ref):
    nq = q_ref.shape[1] // CHUNK
    lanes = 2 * SB_DH
    q_rows = lambda start, size: q_ref[0, pl.ds(start, size), :]
    k_rows = lambda start, size: k_ref[0, pl.ds(start, size), :]
    v_rows = lambda start, size: v_ref[0, pl.ds(start, size), :]
    meta_k = lambda: mk_ref[...]
    meta_v = lambda: mv_ref[...]
    head0_q = lax.broadcasted_iota(jnp.int32, (CHUNK, lanes), 1) < SB_DH
    t_loc = lax.broadcasted_iota(jnp.int32, (2 * CHUNK, CHUNK), 0) & (CHUNK - 1)
    s_loc = lax.broadcasted_iota(jnp.int32, (2 * CHUNK, CHUNK), 1)
    causal = s_loc < t_loc
    tri = tri_ref[...]

    def stacked_q(row0):
        qf = q_rows(row0, CHUNK).astype(F32) * (SB_DH ** -0.5 * LOG2E)
        both = jnp.concatenate([jnp.where(head0_q, qf, 0.0), jnp.where(head0_q, 0.0, qf)], axis=0)
        return both.astype(BF16)

    mask = lambda x, m: x if m is None else jnp.where(m, x, 0.0)

    def scores(qs, kw):
        return lax.dot_general(qs, kw, (((1,), (1,)), ((), ())), preferred_element_type=F32)

    def log_sticks(z, valid_lo, valid_hi):
        sp = _softplus2(z)
        ln = jnp.concatenate([mask(sp[:, :CHUNK], valid_lo), mask(sp[:, CHUNK:], valid_hi)], axis=1)
        return z - sp, ln[:, 0:1], ln.astype(BF16)

    def later_sums(ln):
        return jnp.dot(ln, tri, preferred_element_type=F32)

    def weights(log_beta, ln0, later, valid_lo, valid_hi, r):
        total = later[:, 0:1] + ln0
        if r is not None:
            later = later + r
            total = total + r
        a = jnp.exp2(log_beta - later)
        a = jnp.concatenate([mask(a[:, :CHUNK], valid_lo), mask(a[:, CHUNK:], valid_hi)], axis=1)
        return a.astype(BF16), total

    def attend(a, vw):
        both = jnp.dot(a, vw, preferred_element_type=F32)
        return jnp.where(head0_q, both[:CHUNK], both[CHUNK:])

    def projection_pieces(k):
        rows = pl.ds(pl.multiple_of(k * CHUNK, CHUNK), CHUNK)

        def piece(c):
            cols = slice(c * SB_FILL_COLS, (c + 1) * SB_FILL_COLS)
            ro_ref[0, rows, cols] = jnp.dot(yr_ref[0, rows, :], wr_ref[:, cols], preferred_element_type=F32)
        return [functools.partial(piece, c) for c in range(D_MODEL // SB_FILL_COLS)]

    def windows(qs, kw, vw, valid_lo, valid_hi, r, fillers=()):
        n = len(qs)
        z, st, later, rs, outs = {}, {}, {}, [None] * n, [None] * n
        for t in range(n + 3):
            if t < n:
                z[t] = scores(qs[t], kw[t])
            if 0 <= t - 1 < n:
                st[t - 1] = log_sticks(z.pop(t - 1), valid_lo, valid_hi)
            if 0 <= t - 2 < n:
                later[t - 2] = later_sums(st[t - 2][2])
            if 0 <= t - 3 < n:
                u = t - 3
                a, rs[u] = weights(st[u][0], st[u][1], later.pop(u), valid_lo, valid_hi, r[u])
                outs[u] = attend(a, vw[u])
            if t < len(fillers):
                fillers[t]()
        assert len(fillers) <= n + 3
        return rs, outs

    def diagonal_windows(blocks, first, fillers=()):
        row0 = [pl.multiple_of(i * CHUNK, CHUNK) for i in blocks]
        qs = [stacked_q(r0) for r0 in row0]
        if first:
            kw = [jnp.concatenate([meta_k(), k_rows(0, CHUNK)], axis=0)]
            vw = [jnp.concatenate([meta_v(), v_rows(0, CHUNK)], axis=0)]
            valid_lo = s_loc >= PAD
        else:
            start = [pl.multiple_of(r0 - CHUNK, CHUNK) for r0 in row0]
            kw = [k_rows(s, SB_WIN) for s in start]
            vw = [v_rows(s, SB_WIN) for s in start]
            valid_lo = None
        r, out = windows(qs, kw, vw, valid_lo, causal, [None] * len(blocks), fillers)
        return qs, r, out

    def load_block(rows, meta, j):
        start = pl.multiple_of(jnp.maximum(j, 0) * CHUNK, CHUNK)
        return jnp.where(j < 0, meta(), rows(start, CHUNK))

    def earlier_windows(i, qs, r, acc):
        def window(state):
            j_hi, r, acc = state
            j_lo = j_hi - 1
            kw = jnp.concatenate([load_block(k_rows, meta_k, j_lo), load_block(k_rows, meta_k, j_hi)], axis=0)
            vw = jnp.concatenate([load_block(v_rows, meta_v, j_lo), load_block(v_rows, meta_v, j_hi)], axis=0)
            valid_lo = j_lo * CHUNK + s_loc >= -N_META
            valid_hi = j_hi * CHUNK + s_loc >= -N_META
            r, out = windows([qs], [kw], [vw], valid_lo, valid_hi, [r])
            return j_hi - 2, r[0], acc + out[0]

        def more(state):
            j_hi, r, _ = state
            return (j_hi >= -1) & (jnp.min(r) < SB_STOP)

        return lax.while_loop(more, window, (i - 2, r, acc))[2]

    _, _, out = diagonal_windows([0], True, projection_pieces(0))
    o_ref[0, 0:CHUNK, :] = out[0].astype(BF16)

    def store(i, acc):
        o_ref[0, pl.ds(pl.multiple_of(i * CHUNK, CHUNK), CHUNK), :] = acc.astype(BF16)

    def finish_group(g):
        def block(u, carry):
            r = r_ref[g & 1, u]

            @pl.when(jnp.min(r) < SB_STOP)
            def _():
                i = 1 + g * SB_GROUP + u
                row0 = pl.multiple_of(i * CHUNK, CHUNK)
                acc = o_ref[0, pl.ds(row0, CHUNK), :].astype(F32)
                store(i, earlier_windows(i, stacked_q(row0), r, acc))
            return carry
        lax.fori_loop(0, SB_GROUP, block, 0)

    def group(g, r_prev):
        prev_unfinished = jnp.min(r_prev) < SB_STOP
        blocks = [1 + g * SB_GROUP + u for u in range(SB_GROUP)]
        _, r, out = diagonal_windows(blocks, False, projection_pieces(g + 1))
        for u, i in enumerate(blocks):
            store(i, out[u])
            r_ref[g & 1, u] = r[u]

        @pl.when(prev_unfinished)
        def _():
            finish_group(g - 1)

        return functools.reduce(jnp.minimum, r)

    assert (nq - 1) % SB_GROUP == 0
    n_groups = (nq - 1) // SB_GROUP
    assert (n_groups + 1) * CHUNK == yr_ref.shape[1]
    r_last = lax.fori_loop(0, n_groups, group, jnp.full((2 * CHUNK, 1), SB_STOP, F32))

    @pl.when(jnp.min(r_last) < SB_STOP)
    def _():
        finish_group(n_groups - 1)


def _sb_tables():
    j = np.arange(SB_WIN)[:, None]
    s = np.arange(SB_WIN)[None, :]
    return jnp.asarray((j > s).astype(np.float32), BF16)


def _stickbreak(proj, mproj, yret, wr, seq):
    batch = proj.shape[0]
    lanes = 2 * SB_DH
    pairs = SB_HEADS // 2
    slab = seq // pairs
    return pl.pallas_call(
        _sb_kernel,
        grid=(batch, pairs),
        in_specs=[
            pl.BlockSpec((1, seq, lanes), lambda b, p: (b, 0, (OFF_SQ - RET_COLS) // lanes + p)),
            pl.BlockSpec((1, seq, lanes), lambda b, p: (b, 0, (OFF_SK - RET_COLS) // lanes + p)),
            pl.BlockSpec((1, seq, lanes), lambda b, p: (b, 0, (OFF_SV - RET_COLS) // lanes + p)),
            pl.BlockSpec((CHUNK, lanes), lambda b, p: (0, (OFF_SK - RET_COLS) // lanes + p)),
            pl.BlockSpec((CHUNK, lanes), lambda b, p: (0, (OFF_SV - RET_COLS) // lanes + p)),
            pl.BlockSpec((SB_WIN, SB_WIN), lambda b, p: (0, 0)),
            pl.BlockSpec((1, slab, RET_HEADS * RET_DV), lambda b, p: (b, p, 0)),
            pl.BlockSpec(wr.shape, lambda b, p: (0, 0), pipeline_mode=pl.Buffered(1)),
        ],
        out_specs=[
            pl.BlockSpec((1, seq, lanes), lambda b, p: (b, 0, p)),
            pl.BlockSpec((1, slab, D_MODEL), lambda b, p: (b, p, 0)),
        ],
        out_shape=[
            jax.ShapeDtypeStruct((batch, seq, SB_HEADS * SB_DH), BF16),
            jax.ShapeDtypeStruct((batch, seq, D_MODEL), F32),
        ],
        scratch_shapes=[pltpu.VMEM((2, SB_GROUP, 2 * CHUNK, 1), F32)],
        compiler_params=pltpu.CompilerParams(
            dimension_semantics=("parallel", "parallel"),
            vmem_limit_bytes=VMEM_LIMIT),
        name="stickbreak",
    )(proj, proj, proj, mproj, mproj, _sb_tables(), yret, wr)


FFN_COLS = 256
POST_PARTS = 2


def _post_kernel(r_ref, ys_ref, ga_ref, gb_ref, x_ref, ws_ref, wo_ref, gmix_ref,
                 gpre_ref, wi_ref, wf_ref, gpost_ref, o_ref, u_ref):
    tm = o_ref.shape[0]
    parts = [slice(p * tm // POST_PARTS, (p + 1) * tm // POST_PARTS) for p in range(POST_PARTS)]
    dot = functools.partial(jnp.dot, preferred_element_type=F32)

    r = [r_ref[p, :] for p in parts]
    s = [dot(ys_ref[p, :], ws_ref[...]) for p in parts]
    mix = []
    for i, p in enumerate(parts):
        merged = (jax.nn.sigmoid(ga_ref[p, :].astype(F32)) * r[i]
                  + jax.nn.sigmoid(gb_ref[p, :].astype(F32)) * s[i])
        mix.append(dot(merged.astype(BF16), wo_ref[...]))
    h, hn = [], []
    for i, p in enumerate(parts):
        h.append(x_ref[p, :] + _rms(mix[i], gmix_ref[...]))
        hn.append(_rms(h[i], gpre_ref[...]).astype(BF16))
    for c in range(D_FF // FFN_COLS):
        lo = c * FFN_COLS
        for i, p in enumerate(parts):
            a = dot(hn[i], wi_ref[:, lo:lo + FFN_COLS])
            b = dot(hn[i], wi_ref[:, D_FF + lo:D_FF + lo + FFN_COLS])
            u_ref[p, lo:lo + FFN_COLS] = (a * jax.nn.sigmoid(a) * b).astype(BF16)
    ff = [dot(u_ref[p, :], wf_ref[...]) for p in parts]
    for i, p in enumerate(parts):
        o_ref[p, :] = h[i] + _rms(ff[i], gpost_ref[...])


def _post(r, ysb, proj2d, x2d, ws, wo, gmix, gpre, wi, wf, gpost, tm=512):
    rows = x2d.shape[0]
    row_block = lambda width, col=0: pl.BlockSpec((tm, width), lambda i: (i, col))
    return pl.pallas_call(
        _post_kernel,
        grid=(rows // tm,),
        in_specs=[
            row_block(D_MODEL),
            row_block(D_MODEL),
            row_block(D_MODEL, (OFF_GA - RET_COLS) // D_MODEL),
            row_block(D_MODEL, (OFF_GB - RET_COLS) // D_MODEL),
            row_block(D_MODEL),
            _resident(ws.shape), _resident(wo.shape),
            _resident((1, D_MODEL)), _resident((1, D_MODEL)),
            _resident(wi.shape), _resident(wf.shape),
            _resident((1, D_MODEL)),
        ],
        out_specs=row_block(D_MODEL),
        out_shape=jax.ShapeDtypeStruct((rows, D_MODEL), F32),
        scratch_shapes=[pltpu.VMEM((tm, D_FF), BF16)],
        compiler_params=pltpu.CompilerParams(
            dimension_semantics=("parallel",),
            vmem_limit_bytes=VMEM_LIMIT),
        name="post",
    )(r, ysb, proj2d, proj2d, x2d, ws, wo, gmix, gpre, wi, wf, gpost)


def kernel(x, meta_tokens, w_in, w_ret_out, w_sb_out, w_out, w_ffn_in, w_ffn_out,
           norm_mix_pre, norm_mix_post, norm_ffn_pre, norm_ffn_post):
    batch, seq, d = x.shape
    assert d == D_MODEL and seq % 1024 == 0 and w_in.shape[0] == 1
    x2d = x.reshape(batch * seq, d)
    meta_chunk = jnp.concatenate(
        [jnp.zeros((PAD, d), x.dtype), meta_tokens.astype(x.dtype)], axis=0)

    proj2d, mproj, yret, wr, ws, wo, wi, wf = _front(
        x2d, meta_chunk, norm_mix_pre, w_in, seq,
        [w_ret_out, w_sb_out, w_out, w_ffn_in, w_ffn_out])
    ysb, r = _stickbreak(proj2d.reshape(batch, seq, OUT_COLS), mproj,
                         yret.reshape(batch, seq, -1), wr, seq)

    out = _post(r.reshape(batch * seq, -1), ysb.reshape(batch * seq, -1), proj2d, x2d, ws, wo,
                norm_mix_post, norm_ffn_pre, wi, wf, norm_ffn_post)
    return out.reshape(batch, seq, d)
```

```python
import functools

import jax
import jax.numpy as jnp
import numpy as np
from jax import lax
from jax.experimental import pallas as pl
from jax.experimental.pallas import tpu as pltpu

D_MODEL = 1024
N_META = 16
CHUNK = 128
PAD = CHUNK - N_META
RET_HEADS = 4
RET_DK = 256
RET_DV = 512
SB_HEADS = 16
SB_DH = 64
D_FF = 2816
ROPE_BASE = 10000.0
NORM_EPS = 1e-6
GN_EPS = 1e-5

OFF_RQ = 0
OFF_RK = 1024
OFF_RV = 2048
OFF_RG = 4096
OFF_SQ = 6144
OFF_SK = 7168
OFF_SV = 8192
OFF_GA = 9216
OFF_GB = 10240
PROJ_WIDTH = 11264

F32 = jnp.float32
BF16 = jnp.bfloat16

LOG2E = 1.4426950408889634
SB_STOP = 127.0
SB_WIN = 2 * CHUNK

VMEM_LIMIT = 60 * 1024 * 1024


def _rms(xf, g):
    ms = jnp.mean(xf * xf, axis=-1, keepdims=True)
    return xf * lax.rsqrt(ms + NORM_EPS) * g


def _resident(shape):
    return pl.BlockSpec(shape, lambda i: (0,) * len(shape), pipeline_mode=pl.Buffered(1))


FRONT_ROWS = 256
FRONT_COLS = 1024
W_CHUNK = 512
RET_COLS = OFF_SQ
OUT_COLS = PROJ_WIDTH - RET_COLS


def _rotate(x, cos, sin):
    half = RET_DK // 2
    x1, x2 = x[:, :half], x[:, half:]
    return jnp.concatenate([x1 * cos - x2 * sin, x1 * sin + x2 * cos], axis=1)


def _kv_state(k_rot, v, zeta):
    kz = (k_rot * zeta).astype(BF16)
    return lax.dot_general(kz, v, (((0,), (0,)), ((), ())), preferred_element_type=F32)


def _front_kernel(tiles_per_batch, n_cast,
                  x_ref, m_ref, g_ref, w_ref, cbase_ref, sbase_ref, coff_ref, soff_ref,
                  mcos_ref, msin_ref, mzeta_ref, decay_ref, xi_ref, zeta_ref, gchunk_ref, *rest):
    cast_in, (o_ref, mo_ref, y_ref), rest = rest[:n_cast], rest[n_cast:n_cast + 3], rest[n_cast + 3:]
    cast_out, (ring_ref, state_ref, mret_ref, wb_ref, stage_ref, sem) = rest[:n_cast], rest[n_cast:]
    j = pl.program_id(0)

    for src, dst in zip(cast_in, cast_out):
        dst[...] = src[0].astype(BF16)
    cur = j % 2
    prev = 1 - cur
    n_col = PROJ_WIDTH // FRONT_COLS
    n_ret = RET_COLS // FRONT_COLS

    def normed(rows):
        return _rms(rows, g_ref[...]).astype(BF16)

    def project(hn, put):
        for t in range(n_col):
            cols = slice(t * FRONT_COLS, (t + 1) * FRONT_COLS)
            put(t, jnp.dot(hn, wb_ref[:, cols], preferred_element_type=F32).astype(BF16))
            yield t

    @pl.when(j == 0)
    def _():
        def chunk(c, slot):
            cols = pl.ds(c * W_CHUNK, W_CHUNK)
            return pltpu.make_async_copy(w_ref.at[0, :, cols], stage_ref.at[slot], sem.at[slot])
        n_chunks = PROJ_WIDTH // W_CHUNK
        chunk(0, 0).start()
        for c in range(n_chunks):
            slot = c % 2
            if c + 1 < n_chunks:
                chunk(c + 1, 1 - slot).start()
            chunk(c, slot).wait()
            wb_ref[:, c * W_CHUNK:(c + 1) * W_CHUNK] = stage_ref[slot].astype(BF16)

        ring_ref[1] = jnp.zeros(ring_ref.shape[1:], BF16)
        state_ref[...] = jnp.zeros(state_ref.shape, F32)

        def put_meta(t, res):
            lo = t * FRONT_COLS
            if OFF_RK <= lo < OFF_RG:
                mret_ref[:, lo - OFF_RK:lo - OFF_RK + FRONT_COLS] = res
            elif lo >= RET_COLS:
                mo_ref[:, lo - RET_COLS:lo - RET_COLS + FRONT_COLS] = res
        for _ in project(normed(m_ref[...]), put_meta):
            pass

    @pl.when((j - 1) % tiles_per_batch == 0)
    def _():
        mcos, msin = mcos_ref[...], msin_ref[...]
        for h in range(RET_HEADS):
            k = mret_ref[:, h * RET_DK:(h + 1) * RET_DK].astype(F32)
            v = mret_ref[:, OFF_RV - OFF_RK + h * RET_DV:OFF_RV - OFF_RK + (h + 1) * RET_DV]
            state_ref[h] = _kv_state(_rotate(k, mcos, msin), v, mzeta_ref[h])

    co, so = coff_ref[...], soff_ref[...]
    state = [state_ref[h] for h in range(RET_HEADS)]
    trig = {}
    held = {}

    def phase1(h):
        if not trig:
            cb, sb = cbase_ref[0], sbase_ref[0]
            trig[0] = (cb * co - sb * so, sb * co + cb * so)
        cos, sin = trig[0]
        q = ring_ref[prev, :, OFF_RQ + h * RET_DK:OFF_RQ + (h + 1) * RET_DK].astype(F32)
        k = ring_ref[prev, :, OFF_RK + h * RET_DK:OFF_RK + (h + 1) * RET_DK].astype(F32)
        v = ring_ref[prev, :, OFF_RV + h * RET_DV:OFF_RV + (h + 1) * RET_DV]
        q_rot = (_rotate(q, cos, sin) * (RET_DK ** -0.5)).astype(BF16)
        k_rot = _rotate(k, cos, sin)
        scores = lax.dot_general(q_rot, k_rot.astype(BF16), (((1,), (1,)), ((), ())),
                                 preferred_element_type=F32)
        held[h] = ((scores * decay_ref[h]).astype(BF16), q_rot, (k_rot * zeta_ref[h]).astype(BF16), v)

    def phase2(h):
        sd, q_rot, kz, v = held.pop(h)
        inner = jnp.dot(sd, v, preferred_element_type=F32)
        cross = jnp.dot(q_rot, state[h].astype(BF16), preferred_element_type=F32)
        kv = lax.dot_general(kz, v, (((0,), (0,)), ((), ())), preferred_element_type=F32)
        state[h] = gchunk_ref[h, 0:1, :] * state[h] + kv
        y = inner + cross * xi_ref[h]
        mu = jnp.mean(y, axis=-1, keepdims=True)
        yc = y - mu
        var = jnp.mean(yc * yc, axis=-1, keepdims=True)
        yn = yc * lax.rsqrt(var + GN_EPS)
        gate = ring_ref[prev, :, OFF_RG + h * RET_DV:OFF_RG + (h + 1) * RET_DV].astype(F32)
        y_ref[:, h * RET_DV:(h + 1) * RET_DV] = (gate * jax.nn.sigmoid(gate) * yn).astype(BF16)

    def put(t, res):
        lo = t * FRONT_COLS
        if t < n_ret:
            ring_ref[cur, :, lo:lo + FRONT_COLS] = res
        else:
            o_ref[:, lo - RET_COLS:lo - RET_COLS + FRONT_COLS] = res

    for t in project(normed(x_ref[...]), put):
        if t < 2 * RET_HEADS:
            (phase1 if t % 2 == 0 else phase2)(t // 2)
    assert 2 * RET_HEADS <= n_col

    for h in range(RET_HEADS):
        state_ref[h] = state[h]


def _retention_tables(chunk):
    hh = np.arange(RET_HEADS, dtype=np.float64)
    log_g = np.log1p(-(2.0 ** (-5.0 - hh)))
    idx = np.arange(chunk, dtype=np.float64)
    diff = idx[:, None] - idx[None, :]
    decay = np.where(diff >= 0, np.exp(log_g[:, None, None] * np.maximum(diff, 0.0)), 0.0)
    zeta = np.exp(log_g[:, None] * (chunk - 1.0 - idx))[:, :, None]
    xi = np.exp(log_g[:, None] * (idx + 1.0))[:, :, None]
    gchunk = np.exp(log_g * chunk)[:, None, None]
    bc = lambda a, shape: jnp.asarray(np.broadcast_to(a, shape), F32)
    return (bc(decay, (RET_HEADS, chunk, chunk)),
            bc(xi, (RET_HEADS, chunk, RET_DV)),
            bc(zeta, (RET_HEADS, chunk, RET_DK)),
            bc(gchunk, (RET_HEADS, 8, RET_DV)))


def _rope_tables(seq, chunk):
    half = RET_DK // 2
    inv = ROPE_BASE ** (-np.arange(half, dtype=np.float64) / half)
    base = (N_META + chunk * np.arange(seq // chunk, dtype=np.float64))[:, None, None] * inv
    off = np.arange(chunk, dtype=np.float64)[:, None] * inv
    meta = (np.arange(CHUNK, dtype=np.float64) - PAD)[:, None] * inv
    f = lambda a: jnp.asarray(a, F32)
    return (f(np.cos(base)), f(np.sin(base)), f(np.cos(off)), f(np.sin(off)),
            f(np.cos(meta)), f(np.sin(meta)))


def _cast_slab(rows, steps):
    slab = next(s for s in range(16, rows + 1, 16) if rows % s == 0 and rows // s <= steps)
    return slab, rows // slab


def _front(x2d, meta_chunk, gain, w_in, seq, to_cast):
    rows = x2d.shape[0]
    tm = FRONT_ROWS
    n_tiles = rows // tm
    tiles_per_batch = seq // tm
    cbase, sbase, coff, soff, mcos, msin = _rope_tables(seq, tm)
    decay, xi, zeta, gchunk = _retention_tables(tm)
    mzeta = _retention_tables(CHUNK)[2]
    this_tile = lambda j: (jnp.minimum(j, n_tiles - 1), 0)
    last_tile = lambda j: (jnp.maximum(j - 1, 0), 0)
    last_base = lambda j: (jnp.maximum(j - 1, 0) % tiles_per_batch, 0, 0)
    cast_in_specs, cast_out_specs = [], []
    for w in to_cast:
        slab, blocks = _cast_slab(w.shape[1], n_tiles)
        cast_in_specs.append(pl.BlockSpec(
            (1, slab, w.shape[2]), lambda j, blocks=blocks: (0, jnp.minimum(j, blocks - 1), 0)))
        cast_out_specs.append(pl.BlockSpec(
            (slab, w.shape[2]), lambda j, blocks=blocks: (jnp.minimum(j, blocks - 1), 0)))
    return pl.pallas_call(
        functools.partial(_front_kernel, tiles_per_batch, len(to_cast)),
        grid=(n_tiles + 1,),
        in_specs=[
            pl.BlockSpec((tm, D_MODEL), this_tile),
            _resident((CHUNK, D_MODEL)),
            _resident((1, D_MODEL)),
            pl.BlockSpec(memory_space=pl.ANY),
            pl.BlockSpec((1, 1, RET_DK // 2), last_base),
            pl.BlockSpec((1, 1, RET_DK // 2), last_base),
            _resident((tm, RET_DK // 2)), _resident((tm, RET_DK // 2)),
            _resident((CHUNK, RET_DK // 2)), _resident((CHUNK, RET_DK // 2)),
            _resident((RET_HEADS, CHUNK, RET_DK)),
            _resident((RET_HEADS, tm, tm)),
            _resident((RET_HEADS, tm, RET_DV)),
            _resident((RET_HEADS, tm, RET_DK)),
            _resident((RET_HEADS, 8, RET_DV)),
        ] + cast_in_specs,
        out_specs=[
            pl.BlockSpec((tm, OUT_COLS), this_tile),
            pl.BlockSpec((CHUNK, OUT_COLS), lambda j: (0, 0)),
            pl.BlockSpec((tm, RET_HEADS * RET_DV), last_tile),
        ] + cast_out_specs,
        out_shape=[
            jax.ShapeDtypeStruct((rows, OUT_COLS), BF16),
            jax.ShapeDtypeStruct((CHUNK, OUT_COLS), BF16),
            jax.ShapeDtypeStruct((rows, RET_HEADS * RET_DV), BF16),
        ] + [jax.ShapeDtypeStruct(w.shape[1:], BF16) for w in to_cast],
        scratch_shapes=[
            pltpu.VMEM((2, tm, RET_COLS), BF16),
            pltpu.VMEM((RET_HEADS, RET_DK, RET_DV), F32),
            pltpu.VMEM((CHUNK, OFF_RG - OFF_RK), BF16),
            pltpu.VMEM((D_MODEL, PROJ_WIDTH), BF16),
            pltpu.VMEM((2, D_MODEL, W_CHUNK), F32),
            pltpu.SemaphoreType.DMA((2,)),
        ],
        compiler_params=pltpu.CompilerParams(
            dimension_semantics=("arbitrary",),
            vmem_limit_bytes=VMEM_LIMIT),
        name="front",
    )(x2d, meta_chunk, gain, w_in, cbase, sbase, coff, soff, mcos, msin, mzeta, decay, xi, zeta, gchunk,
      *to_cast)


SB_GROUP = 9
SB_FILL_COLS = 256


def _softplus2(z):
    return jnp.maximum(z, jnp.log2(1.0 + jnp.exp2(jnp.minimum(z, 126.0))))


def _sb_kernel(q_ref, k_ref, v_ref, mk_ref, mv_ref, tri_ref, yr_ref, wr_ref, o_ref, ro_ref, r_ref):
    nq = q_ref.shape[1] // CHUNK
    lanes = 2 * SB_DH
    q_rows = lambda start, size: q_ref[0, pl.ds(start, size), :]
    k_rows = lambda start, size: k_ref[0, pl.ds(start, size), :]
    v_rows = lambda start, size: v_ref[0, pl.ds(start, size), :]
    meta_k = lambda: mk_ref[...]
    meta_v = lambda: mv_ref[...]
    head0_q = lax.broadcasted_iota(jnp.int32, (CHUNK, lanes), 1) < SB_DH
    t_loc = lax.broadcasted_iota(jnp.int32, (2 * CHUNK, CHUNK), 0) & (CHUNK - 1)
    s_loc = lax.broadcasted_iota(jnp.int32, (2 * CHUNK, CHUNK), 1)
    causal = s_loc < t_loc
    tri = tri_ref[...]

    def stacked_q(row0):
        qf = q_rows(row0, CHUNK).astype(F32) * (SB_DH ** -0.5 * LOG2E)
        both = jnp.concatenate([jnp.where(head0_q, qf, 0.0), jnp.where(head0_q, 0.0, qf)], axis=0)
        return both.astype(BF16)

    mask = lambda x, m: x if m is None else jnp.where(m, x, 0.0)

    def scores(qs, kw):
        return lax.dot_general(qs, kw, (((1,), (1,)), ((), ())), preferred_element_type=F32)

    def log_sticks(z, valid_lo, valid_hi):
        sp = _softplus2(z)
        ln = jnp.concatenate([mask(sp[:, :CHUNK], valid_lo), mask(sp[:, CHUNK:], valid_hi)], axis=1)
        return z - sp, ln[:, 0:1], ln.astype(BF16)

    def later_sums(ln):
        return jnp.dot(ln, tri, preferred_element_type=F32)

    def weights(log_beta, ln0, later, valid_lo, valid_hi, r):
        total = later[:, 0:1] + ln0
        if r is not None:
            later = later + r
            total = total + r
        a = jnp.exp2(log_beta - later)
        a = jnp.concatenate([mask(a[:, :CHUNK], valid_lo), mask(a[:, CHUNK:], valid_hi)], axis=1)
        return a.astype(BF16), total

    def attend(a, vw):
        both = jnp.dot(a, vw, preferred_element_type=F32)
        return jnp.where(head0_q, both[:CHUNK], both[CHUNK:])

    def projection_pieces(k):
        rows = pl.ds(pl.multiple_of(k * CHUNK, CHUNK), CHUNK)

        def piece(c):
            cols = slice(c * SB_FILL_COLS, (c + 1) * SB_FILL_COLS)
            ro_ref[0, rows, cols] = jnp.dot(yr_ref[0, rows, :], wr_ref[:, cols], preferred_element_type=F32)
        return [functools.partial(piece, c) for c in range(D_MODEL // SB_FILL_COLS)]

    def windows(qs, kw, vw, valid_lo, valid_hi, r, fillers=()):
        n = len(qs)
        z, st, later, rs, outs = {}, {}, {}, [None] * n, [None] * n
        for t in range(n + 3):
            if t < n:
                z[t] = scores(qs[t], kw[t])
            if 0 <= t - 1 < n:
                st[t - 1] = log_sticks(z.pop(t - 1), valid_lo, valid_hi)
            if 0 <= t - 2 < n:
                later[t - 2] = later_sums(st[t - 2][2])
            if 0 <= t - 3 < n:
                u = t - 3
                a, rs[u] = weights(st[u][0], st[u][1], later.pop(u), valid_lo, valid_hi, r[u])
                outs[u] = attend(a, vw[u])
            if t < len(fillers):
                fillers[t]()
        assert len(fillers) <= n + 3
        return rs, outs

    def diagonal_windows(blocks, first, fillers=()):
        row0 = [pl.multiple_of(i * CHUNK, CHUNK) for i in blocks]
        qs = [stacked_q(r0) for r0 in row0]
        if first:
            kw = [jnp.concatenate([meta_k(), k_rows(0, CHUNK)], axis=0)]
            vw = [jnp.concatenate([meta_v(), v_rows(0, CHUNK)], axis=0)]
            valid_lo = s_loc >= PAD
        else:
            start = [pl.multiple_of(r0 - CHUNK, CHUNK) for r0 in row0]
            kw = [k_rows(s, SB_WIN) for s in start]
            vw = [v_rows(s, SB_WIN) for s in start]
            valid_lo = None
        r, out = windows(qs, kw, vw, valid_lo, causal, [None] * len(blocks), fillers)
        return qs, r, out

    def load_block(rows, meta, j):
        start = pl.multiple_of(jnp.maximum(j, 0) * CHUNK, CHUNK)
        return jnp.where(j < 0, meta(), rows(start, CHUNK))

    def earlier_windows(i, qs, r, acc):
        def window(state):
            j_hi, r, acc = state
            j_lo = j_hi - 1
            kw = jnp.concatenate([load_block(k_rows, meta_k, j_lo), load_block(k_rows, meta_k, j_hi)], axis=0)
            vw = jnp.concatenate([load_block(v_rows, meta_v, j_lo), load_block(v_rows, meta_v, j_hi)], axis=0)
            valid_lo = j_lo * CHUNK + s_loc >= -N_META
            valid_hi = j_hi * CHUNK + s_loc >= -N_META
            r, out = windows([qs], [kw], [vw], valid_lo, valid_hi, [r])
            return j_hi - 2, r[0], acc + out[0]

        def more(state):
            j_hi, r, _ = state
            return (j_hi >= -1) & (jnp.min(r) < SB_STOP)

        return lax.while_loop(more, window, (i - 2, r, acc))[2]

    _, _, out = diagonal_windows([0], True, projection_pieces(0))
    o_ref[0, 0:CHUNK, :] = out[0].astype(BF16)

    def store(i, acc):
        o_ref[0, pl.ds(pl.multiple_of(i * CHUNK, CHUNK), CHUNK), :] = acc.astype(BF16)

    def finish_group(g):
        rs = [r_ref[g & 1, u] for u in range(SB_GROUP)]
        unfinished = [jnp.min(r) < SB_STOP for r in rs]
        for u in range(SB_GROUP):
            @pl.when(unfinished[u])
            def _(u=u):
                i = 1 + g * SB_GROUP + u
                row0 = pl.multiple_of(i * CHUNK, CHUNK)
                acc = o_ref[0, pl.ds(row0, CHUNK), :].astype(F32)
                store(i, earlier_windows(i, stacked_q(row0), rs[u], acc))

    def group(g, r_prev):
        prev_unfinished = jnp.min(r_prev) < SB_STOP
        blocks = [1 + g * SB_GROUP + u for u in range(SB_GROUP)]
        _, r, out = diagonal_windows(blocks, False, projection_pieces(g + 1))
        for u, i in enumerate(blocks):
            store(i, out[u])
            r_ref[g & 1, u] = r[u]

        @pl.when(prev_unfinished)
        def _():
            finish_group(g - 1)

        return functools.reduce(jnp.minimum, r)

    assert (nq - 1) % SB_GROUP == 0
    n_groups = (nq - 1) // SB_GROUP
    assert (n_groups + 1) * CHUNK == yr_ref.shape[1]
    r_last = lax.fori_loop(0, n_groups, group, jnp.full((2 * CHUNK, 1), SB_STOP, F32))

    @pl.when(jnp.min(r_last) < SB_STOP)
    def _():
        finish_group(n_groups - 1)


def _sb_tables():
    j = np.arange(SB_WIN)[:, None]
    s = np.arange(SB_WIN)[None, :]
    return jnp.asarray((j > s).astype(np.float32), BF16)


def _stickbreak(proj, mproj, yret, wr, seq):
    batch = proj.shape[0]
    lanes = 2 * SB_DH
    pairs = SB_HEADS // 2
    slab = seq // pairs
    return pl.pallas_call(
        _sb_kernel,
        grid=(batch, pairs),
        in_specs=[
            pl.BlockSpec((1, seq, lanes), lambda b, p: (b, 0, (OFF_SQ - RET_COLS) // lanes + p)),
            pl.BlockSpec((1, seq, lanes), lambda b, p: (b, 0, (OFF_SK - RET_COLS) // lanes + p)),
            pl.BlockSpec((1, seq, lanes), lambda b, p: (b, 0, (OFF_SV - RET_COLS) // lanes + p)),
            pl.BlockSpec((CHUNK, lanes), lambda b, p: (0, (OFF_SK - RET_COLS) // lanes + p)),
            pl.BlockSpec((CHUNK, lanes), lambda b, p: (0, (OFF_SV - RET_COLS) // lanes + p)),
            pl.BlockSpec((SB_WIN, SB_WIN), lambda b, p: (0, 0)),
            pl.BlockSpec((1, slab, RET_HEADS * RET_DV), lambda b, p: (b, p, 0)),
            pl.BlockSpec(wr.shape, lambda b, p: (0, 0), pipeline_mode=pl.Buffered(1)),
        ],
        out_specs=[
            pl.BlockSpec((1, seq, lanes), lambda b, p: (b, 0, p)),
            pl.BlockSpec((1, slab, D_MODEL), lambda b, p: (b, p, 0)),
        ],
        out_shape=[
            jax.ShapeDtypeStruct((batch, seq, SB_HEADS * SB_DH), BF16),
            jax.ShapeDtypeStruct((batch, seq, D_MODEL), F32),
        ],
        scratch_shapes=[pltpu.VMEM((2, SB_GROUP, 2 * CHUNK, 1), F32)],
        compiler_params=pltpu.CompilerParams(
            dimension_semantics=("parallel", "parallel"),
            vmem_limit_bytes=VMEM_LIMIT),
        name="stickbreak",
    )(proj, proj, proj, mproj, mproj, _sb_tables(), yret, wr)


FFN_COLS = 256
POST_PARTS = 2


def _post_kernel(r_ref, ys_ref, ga_ref, gb_ref, x_ref, ws_ref, wo_ref, gmix_ref,
                 gpre_ref, wi_ref, wf_ref, gpost_ref, o_ref, u_ref):
    tm = o_ref.shape[0]
    parts = [slice(p * tm // POST_PARTS, (p + 1) * tm // POST_PARTS) for p in range(POST_PARTS)]
    dot = functools.partial(jnp.dot, preferred_element_type=F32)

    r = [r_ref[p, :] for p in parts]
    s = [dot(ys_ref[p, :], ws_ref[...]) for p in parts]
    mix = []
    for i, p in enumerate(parts):
        merged = (jax.nn.sigmoid(ga_ref[p, :].astype(F32)) * r[i]
                  + jax.nn.sigmoid(gb_ref[p, :].astype(F32)) * s[i])
        mix.append(dot(merged.astype(BF16), wo_ref[...]))
    h, hn = [], []
    for i, p in enumerate(parts):
        h.append(x_ref[p, :] + _rms(mix[i], gmix_ref[...]))
        hn.append(_rms(h[i], gpre_ref[...]).astype(BF16))
    for c in range(D_FF // FFN_COLS):
        lo = c * FFN_COLS
        for i, p in enumerate(parts):
            a = dot(hn[i], wi_ref[:, lo:lo + FFN_COLS])
            b = dot(hn[i], wi_ref[:, D_FF + lo:D_FF + lo + FFN_COLS])
            u_ref[p, lo:lo + FFN_COLS] = (a * jax.nn.sigmoid(a) * b).astype(BF16)
    ff = [dot(u_ref[p, :], wf_ref[...]) for p in parts]
    for i, p in enumerate(parts):
        o_ref[p, :] = h[i] + _rms(ff[i], gpost_ref[...])


def _post(r, ysb, proj2d, x2d, ws, wo, gmix, gpre, wi, wf, gpost, tm=512):
    rows = x2d.shape[0]
    row_block = lambda width, col=0: pl.BlockSpec((tm, width), lambda i: (i, col))
    return pl.pallas_call(
        _post_kernel,
        grid=(rows // tm,),
        in_specs=[
            row_block(D_MODEL),
            row_block(D_MODEL),
            row_block(D_MODEL, (OFF_GA - RET_COLS) // D_MODEL),
            row_block(D_MODEL, (OFF_GB - RET_COLS) // D_MODEL),
            row_block(D_MODEL),
            _resident(ws.shape), _resident(wo.shape),
            _resident((1, D_MODEL)), _resident((1, D_MODEL)),
            _resident(wi.shape), _resident(wf.shape),
            _resident((1, D_MODEL)),
        ],
        out_specs=row_block(D_MODEL),
        out_shape=jax.ShapeDtypeStruct((rows, D_MODEL), F32),
        scratch_shapes=[pltpu.VMEM((tm, D_FF), BF16)],
        compiler_params=pltpu.CompilerParams(
            dimension_semantics=("parallel",),
            vmem_limit_bytes=VMEM_LIMIT),
        name="post",
    )(r, ysb, proj2d, proj2d, x2d, ws, wo, gmix, gpre, wi, wf, gpost)


def kernel(x, meta_tokens, w_in, w_ret_out, w_sb_out, w_out, w_ffn_in, w_ffn_out,
           norm_mix_pre, norm_mix_post, norm_ffn_pre, norm_ffn_post):
    batch, seq, d = x.shape
    assert d == D_MODEL and seq % 1024 == 0 and w_in.shape[0] == 1
    x2d = x.reshape(batch * seq, d)
    meta_chunk = jnp.concatenate(
        [jnp.zeros((PAD, d), x.dtype), meta_tokens.astype(x.dtype)], axis=0)

    proj2d, mproj, yret, wr, ws, wo, wi, wf = _front(
        x2d, meta_chunk, norm_mix_pre, w_in, seq,
        [w_ret_out, w_sb_out, w_out, w_ffn_in, w_ffn_out])
    ysb, r = _stickbreak(proj2d.reshape(batch, seq, OUT_COLS), mproj,
                         yret.reshape(batch, seq, -1), wr, seq)

    out = _post(r.reshape(batch * seq, -1), ysb.reshape(batch * seq, -1), proj2d, x2d, ws, wo,
                norm_mix_post, norm_ffn_pre, wi, wf, norm_ffn_post)
    return out.reshape(batch, seq, d)
```

```python
import functools

import jax
import jax.numpy as jnp
import numpy as np
from jax import lax
from jax.experimental import pallas as pl
from jax.experimental.pallas import tpu as pltpu

D_MODEL = 1024
N_META = 16
CHUNK = 128
PAD = CHUNK - N_META
RET_HEADS = 4
RET_DK = 256
RET_DV = 512
SB_HEADS = 16
SB_DH = 64
D_FF = 2816
ROPE_BASE = 10000.0
NORM_EPS = 1e-6
GN_EPS = 1e-5

OFF_RQ = 0
OFF_RK = 1024
OFF_RV = 2048
OFF_RG = 4096
OFF_SQ = 6144
OFF_SK = 7168
OFF_SV = 8192
OFF_GA = 9216
OFF_GB = 10240
PROJ_WIDTH = 11264

F32 = jnp.float32
BF16 = jnp.bfloat16

LOG2E = 1.4426950408889634
SB_STOP = 127.0
SB_WIN = 2 * CHUNK

VMEM_LIMIT = 60 * 1024 * 1024


def _rms(xf, g):
    ms = jnp.mean(xf * xf, axis=-1, keepdims=True)
    return xf * lax.rsqrt(ms + NORM_EPS) * g


def _resident(shape):
    return pl.BlockSpec(shape, lambda i: (0,) * len(shape), pipeline_mode=pl.Buffered(1))


FRONT_ROWS = 256
FRONT_COLS = 1024
W_CHUNK = 512
RET_COLS = OFF_SQ
OUT_COLS = PROJ_WIDTH - RET_COLS


def _rotate(x, cos, sin):
    half = RET_DK // 2
    x1, x2 = x[:, :half], x[:, half:]
    return jnp.concatenate([x1 * cos - x2 * sin, x1 * sin + x2 * cos], axis=1)


def _kv_state(k_rot, v, zeta):
    kz = (k_rot * zeta).astype(BF16)
    return lax.dot_general(kz, v, (((0,), (0,)), ((), ())), preferred_element_type=F32)


def _front_kernel(tiles_per_batch, n_cast,
                  x_ref, m_ref, g_ref, w_ref, cbase_ref, sbase_ref, coff_ref, soff_ref,
                  mcos_ref, msin_ref, mzeta_ref, decay_ref, xi_ref, zeta_ref, gchunk_ref, *rest):
    cast_in, (o_ref, mo_ref, y_ref), rest = rest[:n_cast], rest[n_cast:n_cast + 3], rest[n_cast + 3:]
    cast_out, (ring_ref, state_ref, mret_ref, wb_ref, stage_ref, sem) = rest[:n_cast], rest[n_cast:]
    j = pl.program_id(0)

    for src, dst in zip(cast_in, cast_out):
        dst[...] = src[0].astype(BF16)
    cur = j % 2
    prev = 1 - cur
    n_col = PROJ_WIDTH // FRONT_COLS
    n_ret = RET_COLS // FRONT_COLS

    def normed(rows):
        return _rms(rows, g_ref[...]).astype(BF16)

    def project(hn, put):
        for t in range(n_col):
            cols = slice(t * FRONT_COLS, (t + 1) * FRONT_COLS)
            put(t, jnp.dot(hn, wb_ref[:, cols], preferred_element_type=F32).astype(BF16))
            yield t

    @pl.when(j == 0)
    def _():
        def chunk(c, slot):
            cols = pl.ds(c * W_CHUNK, W_CHUNK)
            return pltpu.make_async_copy(w_ref.at[0, :, cols], stage_ref.at[slot], sem.at[slot])
        n_chunks = PROJ_WIDTH // W_CHUNK
        chunk(0, 0).start()
        for c in range(n_chunks):
            slot = c % 2
            if c + 1 < n_chunks:
                chunk(c + 1, 1 - slot).start()
            chunk(c, slot).wait()
            wb_ref[:, c * W_CHUNK:(c + 1) * W_CHUNK] = stage_ref[slot].astype(BF16)

        ring_ref[1] = jnp.zeros(ring_ref.shape[1:], BF16)
        state_ref[...] = jnp.zeros(state_ref.shape, F32)

        def put_meta(t, res):
            lo = t * FRONT_COLS
            if OFF_RK <= lo < OFF_RG:
                mret_ref[:, lo - OFF_RK:lo - OFF_RK + FRONT_COLS] = res
            elif lo >= RET_COLS:
                mo_ref[:, lo - RET_COLS:lo - RET_COLS + FRONT_COLS] = res
        for _ in project(normed(m_ref[...]), put_meta):
            pass

    @pl.when((j - 1) % tiles_per_batch == 0)
    def _():
        mcos, msin = mcos_ref[...], msin_ref[...]
        for h in range(RET_HEADS):
            k = mret_ref[:, h * RET_DK:(h + 1) * RET_DK].astype(F32)
            v = mret_ref[:, OFF_RV - OFF_RK + h * RET_DV:OFF_RV - OFF_RK + (h + 1) * RET_DV]
            state_ref[h] = _kv_state(_rotate(k, mcos, msin), v, mzeta_ref[h])

    co, so = coff_ref[...], soff_ref[...]
    state = [state_ref[h] for h in range(RET_HEADS)]
    trig = {}
    held = {}

    def phase1(h):
        if not trig:
            cb, sb = cbase_ref[0], sbase_ref[0]
            trig[0] = (cb * co - sb * so, sb * co + cb * so)
        cos, sin = trig[0]
        q = ring_ref[prev, :, OFF_RQ + h * RET_DK:OFF_RQ + (h + 1) * RET_DK].astype(F32)
        k = ring_ref[prev, :, OFF_RK + h * RET_DK:OFF_RK + (h + 1) * RET_DK].astype(F32)
        v = ring_ref[prev, :, OFF_RV + h * RET_DV:OFF_RV + (h + 1) * RET_DV]
        q_rot = (_rotate(q, cos, sin) * (RET_DK ** -0.5)).astype(BF16)
        k_rot = _rotate(k, cos, sin)
        scores = lax.dot_general(q_rot, k_rot.astype(BF16), (((1,), (1,)), ((), ())),
                                 preferred_element_type=F32)
        held[h] = ((scores * decay_ref[h]).astype(BF16), q_rot, (k_rot * zeta_ref[h]).astype(BF16), v)

    def phase2(h):
        sd, q_rot, kz, v = held.pop(h)
        inner = jnp.dot(sd, v, preferred_element_type=F32)
        cross = jnp.dot(q_rot, state[h].astype(BF16), preferred_element_type=F32)
        kv = lax.dot_general(kz, v, (((0,), (0,)), ((), ())), preferred_element_type=F32)
        state[h] = gchunk_ref[h, 0:1, :] * state[h] + kv
        y = inner + cross * xi_ref[h]
        mu = jnp.mean(y, axis=-1, keepdims=True)
        yc = y - mu
        var = jnp.mean(yc * yc, axis=-1, keepdims=True)
        yn = yc * lax.rsqrt(var + GN_EPS)
        gate = ring_ref[prev, :, OFF_RG + h * RET_DV:OFF_RG + (h + 1) * RET_DV].astype(F32)
        y_ref[:, h * RET_DV:(h + 1) * RET_DV] = (gate * jax.nn.sigmoid(gate) * yn).astype(BF16)

    def put(t, res):
        lo = t * FRONT_COLS
        if t < n_ret:
            ring_ref[cur, :, lo:lo + FRONT_COLS] = res
        else:
            o_ref[:, lo - RET_COLS:lo - RET_COLS + FRONT_COLS] = res

    for t in project(normed(x_ref[...]), put):
        if t < 2 * RET_HEADS:
            (phase1 if t % 2 == 0 else phase2)(t // 2)
    assert 2 * RET_HEADS <= n_col

    for h in range(RET_HEADS):
        state_ref[h] = state[h]


def _retention_tables(chunk):
    hh = np.arange(RET_HEADS, dtype=np.float64)
    log_g = np.log1p(-(2.0 ** (-5.0 - hh)))
    idx = np.arange(chunk, dtype=np.float64)
    diff = idx[:, None] - idx[None, :]
    decay = np.where(diff >= 0, np.exp(log_g[:, None, None] * np.maximum(diff, 0.0)), 0.0)
    zeta = np.exp(log_g[:, None] * (chunk - 1.0 - idx))[:, :, None]
    xi = np.exp(log_g[:, None] * (idx + 1.0))[:, :, None]
    gchunk = np.exp(log_g * chunk)[:, None, None]
    bc = lambda a, shape: jnp.asarray(np.broadcast_to(a, shape), F32)
    return (bc(decay, (RET_HEADS, chunk, chunk)),
            bc(xi, (RET_HEADS, chunk, RET_DV)),
            bc(zeta, (RET_HEADS, chunk, RET_DK)),
            bc(gchunk, (RET_HEADS, 8, RET_DV)))


def _rope_tables(seq, chunk):
    half = RET_DK // 2
    inv = ROPE_BASE ** (-np.arange(half, dtype=np.float64) / half)
    base = (N_META + chunk * np.arange(seq // chunk, dtype=np.float64))[:, None, None] * inv
    off = np.arange(chunk, dtype=np.float64)[:, None] * inv
    meta = (np.arange(CHUNK, dtype=np.float64) - PAD)[:, None] * inv
    f = lambda a: jnp.asarray(a, F32)
    return (f(np.cos(base)), f(np.sin(base)), f(np.cos(off)), f(np.sin(off)),
            f(np.cos(meta)), f(np.sin(meta)))


def _cast_slab(rows, steps):
    slab = next(s for s in range(16, rows + 1, 16) if rows % s == 0 and rows // s <= steps)
    return slab, rows // slab


def _front(x2d, meta_chunk, gain, w_in, seq, to_cast):
    rows = x2d.shape[0]
    tm = FRONT_ROWS
    n_tiles = rows // tm
    tiles_per_batch = seq // tm
    cbase, sbase, coff, soff, mcos, msin = _rope_tables(seq, tm)
    decay, xi, zeta, gchunk = _retention_tables(tm)
    mzeta = _retention_tables(CHUNK)[2]
    this_tile = lambda j: (jnp.minimum(j, n_tiles - 1), 0)
    last_tile = lambda j: (jnp.maximum(j - 1, 0), 0)
    last_base = lambda j: (jnp.maximum(j - 1, 0) % tiles_per_batch, 0, 0)
    cast_in_specs, cast_out_specs = [], []
    for w in to_cast:
        slab, blocks = _cast_slab(w.shape[1], n_tiles)
        cast_in_specs.append(pl.BlockSpec(
            (1, slab, w.shape[2]), lambda j, blocks=blocks: (0, jnp.minimum(j, blocks - 1), 0)))
        cast_out_specs.append(pl.BlockSpec(
            (slab, w.shape[2]), lambda j, blocks=blocks: (jnp.minimum(j, blocks - 1), 0)))
    return pl.pallas_call(
        functools.partial(_front_kernel, tiles_per_batch, len(to_cast)),
        grid=(n_tiles + 1,),
        in_specs=[
            pl.BlockSpec((tm, D_MODEL), this_tile),
            _resident((CHUNK, D_MODEL)),
            _resident((1, D_MODEL)),
            pl.BlockSpec(memory_space=pl.ANY),
            pl.BlockSpec((1, 1, RET_DK // 2), last_base),
            pl.BlockSpec((1, 1, RET_DK // 2), last_base),
            _resident((tm, RET_DK // 2)), _resident((tm, RET_DK // 2)),
            _resident((CHUNK, RET_DK // 2)), _resident((CHUNK, RET_DK // 2)),
            _resident((RET_HEADS, CHUNK, RET_DK)),
            _resident((RET_HEADS, tm, tm)),
            _resident((RET_HEADS, tm, RET_DV)),
            _resident((RET_HEADS, tm, RET_DK)),
            _resident((RET_HEADS, 8, RET_DV)),
        ] + cast_in_specs,
        out_specs=[
            pl.BlockSpec((tm, OUT_COLS), this_tile),
            pl.BlockSpec((CHUNK, OUT_COLS), lambda j: (0, 0)),
            pl.BlockSpec((tm, RET_HEADS * RET_DV), last_tile),
        ] + cast_out_specs,
        out_shape=[
            jax.ShapeDtypeStruct((rows, OUT_COLS), BF16),
            jax.ShapeDtypeStruct((CHUNK, OUT_COLS), BF16),
            jax.ShapeDtypeStruct((rows, RET_HEADS * RET_DV), BF16),
        ] + [jax.ShapeDtypeStruct(w.shape[1:], BF16) for w in to_cast],
        scratch_shapes=[
            pltpu.VMEM((2, tm, RET_COLS), BF16),
            pltpu.VMEM((RET_HEADS, RET_DK, RET_DV), F32),
            pltpu.VMEM((CHUNK, OFF_RG - OFF_RK), BF16),
            pltpu.VMEM((D_MODEL, PROJ_WIDTH), BF16),
            pltpu.VMEM((2, D_MODEL, W_CHUNK), F32),
            pltpu.SemaphoreType.DMA((2,)),
        ],
        compiler_params=pltpu.CompilerParams(
            dimension_semantics=("arbitrary",),
            vmem_limit_bytes=VMEM_LIMIT),
        name="front",
    )(x2d, meta_chunk, gain, w_in, cbase, sbase, coff, soff, mcos, msin, mzeta, decay, xi, zeta, gchunk,
      *to_cast)


SB_GROUP = 21
SB_FILL_COLS = 256


def _softplus2(z):
    return jnp.maximum(z, jnp.log2(1.0 + jnp.exp2(jnp.minimum(z, 126.0))))


def _sb_kernel(q_ref, k_ref, v_ref, mk_ref, mv_ref, tri_ref, yr_ref, wr_ref, o_ref, ro_ref, r_ref):
    nq = q_ref.shape[1] // CHUNK
    lanes = 2 * SB_DH
    q_rows = lambda start, size: q_ref[0, pl.ds(start, size), :]
    k_rows = lambda start, size: k_ref[0, pl.ds(start, size), :]
    v_rows = lambda start, size: v_ref[0, pl.ds(start, size), :]
    meta_k = lambda: mk_ref[...]
    meta_v = lambda: mv_ref[...]
    head0_q = lax.broadcasted_iota(jnp.int32, (CHUNK, lanes), 1) < SB_DH
    t_loc = lax.broadcasted_iota(jnp.int32, (2 * CHUNK, CHUNK), 0) & (CHUNK - 1)
    s_loc = lax.broadcasted_iota(jnp.int32, (2 * CHUNK, CHUNK), 1)
    causal = s_loc < t_loc
    tri = tri_ref[...]

    def stacked_q(row0):
        qf = q_rows(row0, CHUNK).astype(F32) * (SB_DH ** -0.5 * LOG2E)
        both = jnp.concatenate([jnp.where(head0_q, qf, 0.0), jnp.where(head0_q, 0.0, qf)], axis=0)
        return both.astype(BF16)

    mask = lambda x, m: x if m is None else jnp.where(m, x, 0.0)

    def scores(qs, kw):
        return lax.dot_general(qs, kw, (((1,), (1,)), ((), ())), preferred_element_type=F32)

    def log_sticks(z, valid_lo, valid_hi):
        sp = _softplus2(z)
        ln = jnp.concatenate([mask(sp[:, :CHUNK], valid_lo), mask(sp[:, CHUNK:], valid_hi)], axis=1)
        return z - sp, ln[:, 0:1], ln.astype(BF16)

    def later_sums(ln):
        return jnp.dot(ln, tri, preferred_element_type=F32)

    def weights(log_beta, ln0, later, valid_lo, valid_hi, r):
        total = later[:, 0:1] + ln0
        if r is not None:
            later = later + r
            total = total + r
        a = jnp.exp2(log_beta - later)
        a = jnp.concatenate([mask(a[:, :CHUNK], valid_lo), mask(a[:, CHUNK:], valid_hi)], axis=1)
        return a.astype(BF16), total

    def attend(a, vw):
        both = jnp.dot(a, vw, preferred_element_type=F32)
        return jnp.where(head0_q, both[:CHUNK], both[CHUNK:])

    fill_units = (nq - 1) // SB_GROUP + 1
    fill_chunks = yr_ref.shape[1] // (fill_units * CHUNK)
    assert fill_units * fill_chunks * CHUNK == yr_ref.shape[1]

    def projection_pieces(k):
        def piece(i, c):
            rows = pl.ds(pl.multiple_of((k * fill_chunks + i) * CHUNK, CHUNK), CHUNK)
            cols = slice(c * SB_FILL_COLS, (c + 1) * SB_FILL_COLS)
            ro_ref[0, rows, cols] = jnp.dot(yr_ref[0, rows, :], wr_ref[:, cols], preferred_element_type=F32)
        return [functools.partial(piece, i, c)
                for i in range(fill_chunks) for c in range(D_MODEL // SB_FILL_COLS)]

    def windows(qs, kw, vw, valid_lo, valid_hi, r, fillers=()):
        n = len(qs)
        z, st, later, rs, outs = {}, {}, {}, [None] * n, [None] * n
        for t in range(n + 3):
            if t < n:
                z[t] = scores(qs[t], kw[t])
            if 0 <= t - 1 < n:
                st[t - 1] = log_sticks(z.pop(t - 1), valid_lo, valid_hi)
            if 0 <= t - 2 < n:
                later[t - 2] = later_sums(st[t - 2][2])
            if 0 <= t - 3 < n:
                u = t - 3
                a, rs[u] = weights(st[u][0], st[u][1], later.pop(u), valid_lo, valid_hi, r[u])
                outs[u] = attend(a, vw[u])
            if t < len(fillers):
                fillers[t]()
        for fill in fillers[n + 3:]:
            fill()
        return rs, outs

    def diagonal_windows(blocks, first, fillers=()):
        row0 = [pl.multiple_of(i * CHUNK, CHUNK) for i in blocks]
        qs = [stacked_q(r0) for r0 in row0]
        if first:
            kw = [jnp.concatenate([meta_k(), k_rows(0, CHUNK)], axis=0)]
            vw = [jnp.concatenate([meta_v(), v_rows(0, CHUNK)], axis=0)]
            valid_lo = s_loc >= PAD
        else:
            start = [pl.multiple_of(r0 - CHUNK, CHUNK) for r0 in row0]
            kw = [k_rows(s, SB_WIN) for s in start]
            vw = [v_rows(s, SB_WIN) for s in start]
            valid_lo = None
        r, out = windows(qs, kw, vw, valid_lo, causal, [None] * len(blocks), fillers)
        return qs, r, out

    def load_block(rows, meta, j):
        start = pl.multiple_of(jnp.maximum(j, 0) * CHUNK, CHUNK)
        return jnp.where(j < 0, meta(), rows(start, CHUNK))

    def earlier_windows(i, qs, r, acc):
        def window(state):
            j_hi, r, acc = state
            j_lo = j_hi - 1
            kw = jnp.concatenate([load_block(k_rows, meta_k, j_lo), load_block(k_rows, meta_k, j_hi)], axis=0)
            vw = jnp.concatenate([load_block(v_rows, meta_v, j_lo), load_block(v_rows, meta_v, j_hi)], axis=0)
            valid_lo = j_lo * CHUNK + s_loc >= -N_META
            valid_hi = j_hi * CHUNK + s_loc >= -N_META
            r, out = windows([qs], [kw], [vw], valid_lo, valid_hi, [r])
            return j_hi - 2, r[0], acc + out[0]

        def more(state):
            j_hi, r, _ = state
            return (j_hi >= -1) & (jnp.min(r) < SB_STOP)

        return lax.while_loop(more, window, (i - 2, r, acc))[2]

    _, _, out = diagonal_windows([0], True, projection_pieces(0))
    o_ref[0, 0:CHUNK, :] = out[0].astype(BF16)

    def store(i, acc):
        o_ref[0, pl.ds(pl.multiple_of(i * CHUNK, CHUNK), CHUNK), :] = acc.astype(BF16)

    def finish_group(g):
        rs = [r_ref[g & 1, u] for u in range(SB_GROUP)]
        unfinished = [jnp.min(r) < SB_STOP for r in rs]
        for u in range(SB_GROUP):
            @pl.when(unfinished[u])
            def _(u=u):
                i = 1 + g * SB_GROUP + u
                row0 = pl.multiple_of(i * CHUNK, CHUNK)
                acc = o_ref[0, pl.ds(row0, CHUNK), :].astype(F32)
                store(i, earlier_windows(i, stacked_q(row0), rs[u], acc))

    def group(g, r_prev):
        prev_unfinished = jnp.min(r_prev) < SB_STOP
        blocks = [1 + g * SB_GROUP + u for u in range(SB_GROUP)]
        _, r, out = diagonal_windows(blocks, False, projection_pieces(g + 1))
        for u, i in enumerate(blocks):
            store(i, out[u])
            r_ref[g & 1, u] = r[u]

        @pl.when(prev_unfinished)
        def _():
            finish_group(g - 1)

        return functools.reduce(jnp.minimum, r)

    assert (nq - 1) % SB_GROUP == 0
    n_groups = (nq - 1) // SB_GROUP
    r_last = lax.fori_loop(0, n_groups, group, jnp.full((2 * CHUNK, 1), SB_STOP, F32))

    @pl.when(jnp.min(r_last) < SB_STOP)
    def _():
        finish_group(n_groups - 1)


def _sb_tables():
    j = np.arange(SB_WIN)[:, None]
    s = np.arange(SB_WIN)[None, :]
    return jnp.asarray((j > s).astype(np.float32), BF16)


def _stickbreak(proj, mproj, yret, wr, seq):
    batch = proj.shape[0]
    lanes = 2 * SB_DH
    pairs = SB_HEADS // 2
    slab = seq // pairs
    return pl.pallas_call(
        _sb_kernel,
        grid=(batch, pairs),
        in_specs=[
            pl.BlockSpec((1, seq, lanes), lambda b, p: (b, 0, (OFF_SQ - RET_COLS) // lanes + p)),
            pl.BlockSpec((1, seq, lanes), lambda b, p: (b, 0, (OFF_SK - RET_COLS) // lanes + p)),
            pl.BlockSpec((1, seq, lanes), lambda b, p: (b, 0, (OFF_SV - RET_COLS) // lanes + p)),
            pl.BlockSpec((CHUNK, lanes), lambda b, p: (0, (OFF_SK - RET_COLS) // lanes + p)),
            pl.BlockSpec((CHUNK, lanes), lambda b, p: (0, (OFF_SV - RET_COLS) // lanes + p)),
            pl.BlockSpec((SB_WIN, SB_WIN), lambda b, p: (0, 0)),
            pl.BlockSpec((1, slab, RET_HEADS * RET_DV), lambda b, p: (b, p, 0)),
            pl.BlockSpec(wr.shape, lambda b, p: (0, 0), pipeline_mode=pl.Buffered(1)),
        ],
        out_specs=[
            pl.BlockSpec((1, seq, lanes), lambda b, p: (b, 0, p)),
            pl.BlockSpec((1, slab, D_MODEL), lambda b, p: (b, p, 0)),
        ],
        out_shape=[
            jax.ShapeDtypeStruct((batch, seq, SB_HEADS * SB_DH), BF16),
            jax.ShapeDtypeStruct((batch, seq, D_MODEL), F32),
        ],
        scratch_shapes=[pltpu.VMEM((2, SB_GROUP, 2 * CHUNK, 1), F32)],
        compiler_params=pltpu.CompilerParams(
            dimension_semantics=("parallel", "parallel"),
            vmem_limit_bytes=VMEM_LIMIT),
        name="stickbreak",
    )(proj, proj, proj, mproj, mproj, _sb_tables(), yret, wr)


FFN_COLS = 256
POST_PARTS = 2


def _post_kernel(r_ref, ys_ref, ga_ref, gb_ref, x_ref, ws_ref, wo_ref, gmix_ref,
                 gpre_ref, wi_ref, wf_ref, gpost_ref, o_ref, u_ref):
    tm = o_ref.shape[0]
    parts = [slice(p * tm // POST_PARTS, (p + 1) * tm // POST_PARTS) for p in range(POST_PARTS)]
    dot = functools.partial(jnp.dot, preferred_element_type=F32)

    r = [r_ref[p, :] for p in parts]
    s = [dot(ys_ref[p, :], ws_ref[...]) for p in parts]
    mix = []
    for i, p in enumerate(parts):
        merged = (jax.nn.sigmoid(ga_ref[p, :].astype(F32)) * r[i]
                  + jax.nn.sigmoid(gb_ref[p, :].astype(F32)) * s[i])
        mix.append(dot(merged.astype(BF16), wo_ref[...]))
    h, hn = [], []
    for i, p in enumerate(parts):
        h.append(x_ref[p, :] + _rms(mix[i], gmix_ref[...]))
        hn.append(_rms(h[i], gpre_ref[...]).astype(BF16))
    for c in range(D_FF // FFN_COLS):
        lo = c * FFN_COLS
        for i, p in enumerate(parts):
            a = dot(hn[i], wi_ref[:, lo:lo + FFN_COLS])
            b = dot(hn[i], wi_ref[:, D_FF + lo:D_FF + lo + FFN_COLS])
            u_ref[p, lo:lo + FFN_COLS] = (a * jax.nn.sigmoid(a) * b).astype(BF16)
    ff = [dot(u_ref[p, :], wf_ref[...]) for p in parts]
    for i, p in enumerate(parts):
        o_ref[p, :] = h[i] + _rms(ff[i], gpost_ref[...])


def _post(r, ysb, proj2d, x2d, ws, wo, gmix, gpre, wi, wf, gpost, tm=512):
    rows = x2d.shape[0]
    row_block = lambda width, col=0: pl.BlockSpec((tm, width), lambda i: (i, col))
    return pl.pallas_call(
        _post_kernel,
        grid=(rows // tm,),
        in_specs=[
            row_block(D_MODEL),
            row_block(D_MODEL),
            row_block(D_MODEL, (OFF_GA - RET_COLS) // D_MODEL),
            row_block(D_MODEL, (OFF_GB - RET_COLS) // D_MODEL),
            row_block(D_MODEL),
            _resident(ws.shape), _resident(wo.shape),
            _resident((1, D_MODEL)), _resident((1, D_MODEL)),
            _resident(wi.shape), _resident(wf.shape),
            _resident((1, D_MODEL)),
        ],
        out_specs=row_block(D_MODEL),
        out_shape=jax.ShapeDtypeStruct((rows, D_MODEL), F32),
        scratch_shapes=[pltpu.VMEM((tm, D_FF), BF16)],
        compiler_params=pltpu.CompilerParams(
            dimension_semantics=("parallel",),
            vmem_limit_bytes=VMEM_LIMIT),
        name="post",
    )(r, ysb, proj2d, proj2d, x2d, ws, wo, gmix, gpre, wi, wf, gpost)


def kernel(x, meta_tokens, w_in, w_ret_out, w_sb_out, w_out, w_ffn_in, w_ffn_out,
           norm_mix_pre, norm_mix_post, norm_ffn_pre, norm_ffn_post):
    batch, seq, d = x.shape
    assert d == D_MODEL and seq % 1024 == 0 and w_in.shape[0] == 1
    x2d = x.reshape(batch * seq, d)
    meta_chunk = jnp.concatenate(
        [jnp.zeros((PAD, d), x.dtype), meta_tokens.astype(x.dtype)], axis=0)

    proj2d, mproj, yret, wr, ws, wo, wi, wf = _front(
        x2d, meta_chunk, norm_mix_pre, w_in, seq,
        [w_ret_out, w_sb_out, w_out, w_ffn_in, w_ffn_out])
    ysb, r = _stickbreak(proj2d.reshape(batch, seq, OUT_COLS), mproj,
                         yret.reshape(batch, seq, -1), wr, seq)

    out = _post(r.reshape(batch * seq, -1), ysb.reshape(batch * seq, -1), proj2d, x2d, ws, wo,
                norm_mix_post, norm_ffn_pre, wi, wf, norm_ffn_post)
    return out.reshape(batch, seq, d)
```

```python
import functools

import jax
import jax.numpy as jnp
import numpy as np
from jax import lax
from jax.experimental import pallas as pl
from jax.experimental.pallas import tpu as pltpu

D_MODEL = 1024
N_META = 16
CHUNK = 128
PAD = CHUNK - N_META
RET_HEADS = 4
RET_DK = 256
RET_DV = 512
SB_HEADS = 16
SB_DH = 64
D_FF = 2816
ROPE_BASE = 10000.0
NORM_EPS = 1e-6
GN_EPS = 1e-5

OFF_RQ = 0
OFF_RK = 1024
OFF_RV = 2048
OFF_RG = 4096
OFF_SQ = 6144
OFF_SK = 7168
OFF_SV = 8192
OFF_GA = 9216
OFF_GB = 10240
PROJ_WIDTH = 11264

F32 = jnp.float32
BF16 = jnp.bfloat16

LOG2E = 1.4426950408889634
SB_STOP = 127.0
SB_WIN = 2 * CHUNK

VMEM_LIMIT = 60 * 1024 * 1024


def _rms(xf, g):
    ms = jnp.mean(xf * xf, axis=-1, keepdims=True)
    return xf * lax.rsqrt(ms + NORM_EPS) * g


def _resident(shape):
    return pl.BlockSpec(shape, lambda i: (0,) * len(shape), pipeline_mode=pl.Buffered(1))


FRONT_ROWS = 256
FRONT_COLS = 1024
W_CHUNK = 512
RET_COLS = OFF_SQ
OUT_COLS = PROJ_WIDTH - RET_COLS


def _rotate(x, cos, sin):
    half = RET_DK // 2
    x1, x2 = x[:, :half], x[:, half:]
    return jnp.concatenate([x1 * cos - x2 * sin, x1 * sin + x2 * cos], axis=1)


def _kv_state(k_rot, v, zeta):
    kz = (k_rot * zeta).astype(BF16)
    return lax.dot_general(kz, v, (((0,), (0,)), ((), ())), preferred_element_type=F32)


def _front_kernel(tiles_per_batch, n_cast,
                  x_ref, m_ref, g_ref, w_ref, cbase_ref, sbase_ref, coff_ref, soff_ref,
                  mcos_ref, msin_ref, mzeta_ref, decay_ref, xi_ref, zeta_ref, gchunk_ref, *rest):
    cast_in, (o_ref, mo_ref, y_ref), rest = rest[:n_cast], rest[n_cast:n_cast + 3], rest[n_cast + 3:]
    cast_out, (ring_ref, state_ref, mret_ref, wb_ref, stage_ref, sem) = rest[:n_cast], rest[n_cast:]
    j = pl.program_id(0)

    for src, dst in zip(cast_in, cast_out):
        dst[...] = src[0].astype(BF16)
    cur = j % 2
    prev = 1 - cur
    n_col = PROJ_WIDTH // FRONT_COLS
    n_ret = RET_COLS // FRONT_COLS

    def normed(rows):
        return _rms(rows, g_ref[...]).astype(BF16)

    def project(hn, put):
        for t in range(n_col):
            cols = slice(t * FRONT_COLS, (t + 1) * FRONT_COLS)
            put(t, jnp.dot(hn, wb_ref[:, cols], preferred_element_type=F32).astype(BF16))
            yield t

    @pl.when(j == 0)
    def _():
        def chunk(c, slot):
            cols = pl.ds(c * W_CHUNK, W_CHUNK)
            return pltpu.make_async_copy(w_ref.at[0, :, cols], stage_ref.at[slot], sem.at[slot])
        n_chunks = PROJ_WIDTH // W_CHUNK
        chunk(0, 0).start()
        for c in range(n_chunks):
            slot = c % 2
            if c + 1 < n_chunks:
                chunk(c + 1, 1 - slot).start()
            chunk(c, slot).wait()
            wb_ref[:, c * W_CHUNK:(c + 1) * W_CHUNK] = stage_ref[slot].astype(BF16)

        ring_ref[1] = jnp.zeros(ring_ref.shape[1:], BF16)
        state_ref[...] = jnp.zeros(state_ref.shape, F32)

        def put_meta(t, res):
            lo = t * FRONT_COLS
            if OFF_RK <= lo < OFF_RG:
                mret_ref[:, lo - OFF_RK:lo - OFF_RK + FRONT_COLS] = res
            elif lo >= RET_COLS:
                mo_ref[:, lo - RET_COLS:lo - RET_COLS + FRONT_COLS] = res
        for _ in project(normed(m_ref[...]), put_meta):
            pass

    @pl.when((j - 1) % tiles_per_batch == 0)
    def _():
        mcos, msin = mcos_ref[...], msin_ref[...]
        for h in range(RET_HEADS):
            k = mret_ref[:, h * RET_DK:(h + 1) * RET_DK].astype(F32)
            v = mret_ref[:, OFF_RV - OFF_RK + h * RET_DV:OFF_RV - OFF_RK + (h + 1) * RET_DV]
            state_ref[h] = _kv_state(_rotate(k, mcos, msin), v, mzeta_ref[h])

    co, so = coff_ref[...], soff_ref[...]
    state = [state_ref[h] for h in range(RET_HEADS)]
    trig = {}
    held = {}

    def phase1(h):
        if not trig:
            cb, sb = cbase_ref[0], sbase_ref[0]
            trig[0] = (cb * co - sb * so, sb * co + cb * so)
        cos, sin = trig[0]
        q = ring_ref[prev, :, OFF_RQ + h * RET_DK:OFF_RQ + (h + 1) * RET_DK].astype(F32)
        k = ring_ref[prev, :, OFF_RK + h * RET_DK:OFF_RK + (h + 1) * RET_DK].astype(F32)
        v = ring_ref[prev, :, OFF_RV + h * RET_DV:OFF_RV + (h + 1) * RET_DV]
        q_rot = (_rotate(q, cos, sin) * (RET_DK ** -0.5)).astype(BF16)
        k_rot = _rotate(k, cos, sin)
        scores = lax.dot_general(q_rot, k_rot.astype(BF16), (((1,), (1,)), ((), ())),
                                 preferred_element_type=F32)
        held[h] = ((scores * decay_ref[h]).astype(BF16), q_rot, (k_rot * zeta_ref[h]).astype(BF16), v)

    def phase2(h):
        sd, q_rot, kz, v = held.pop(h)
        inner = jnp.dot(sd, v, preferred_element_type=F32)
        cross = jnp.dot(q_rot, state[h].astype(BF16), preferred_element_type=F32)
        kv = lax.dot_general(kz, v, (((0,), (0,)), ((), ())), preferred_element_type=F32)
        state[h] = gchunk_ref[h, 0:1, :] * state[h] + kv
        y = inner + cross * xi_ref[h]
        mu = jnp.mean(y, axis=-1, keepdims=True)
        yc = y - mu
        var = jnp.mean(yc * yc, axis=-1, keepdims=True)
        yn = yc * lax.rsqrt(var + GN_EPS)
        gate = ring_ref[prev, :, OFF_RG + h * RET_DV:OFF_RG + (h + 1) * RET_DV].astype(F32)
        y_ref[:, h * RET_DV:(h + 1) * RET_DV] = (gate * jax.nn.sigmoid(gate) * yn).astype(BF16)

    def put(t, res):
        lo = t * FRONT_COLS
        if t < n_ret:
            ring_ref[cur, :, lo:lo + FRONT_COLS] = res
        else:
            o_ref[:, lo - RET_COLS:lo - RET_COLS + FRONT_COLS] = res

    for t in project(normed(x_ref[...]), put):
        if t < 2 * RET_HEADS:
            (phase1 if t % 2 == 0 else phase2)(t // 2)
    assert 2 * RET_HEADS <= n_col

    for h in range(RET_HEADS):
        state_ref[h] = state[h]


def _retention_tables(chunk):
    hh = np.arange(RET_HEADS, dtype=np.float64)
    log_g = np.log1p(-(2.0 ** (-5.0 - hh)))
    idx = np.arange(chunk, dtype=np.float64)
    diff = idx[:, None] - idx[None, :]
    decay = np.where(diff >= 0, np.exp(log_g[:, None, None] * np.maximum(diff, 0.0)), 0.0)
    zeta = np.exp(log_g[:, None] * (chunk - 1.0 - idx))[:, :, None]
    xi = np.exp(log_g[:, None] * (idx + 1.0))[:, :, None]
    gchunk = np.exp(log_g * chunk)[:, None, None]
    bc = lambda a, shape: jnp.asarray(np.broadcast_to(a, shape), F32)
    return (bc(decay, (RET_HEADS, chunk, chunk)),
            bc(xi, (RET_HEADS, chunk, RET_DV)),
            bc(zeta, (RET_HEADS, chunk, RET_DK)),
            bc(gchunk, (RET_HEADS, 8, RET_DV)))


def _rope_tables(seq, chunk):
    half = RET_DK // 2
    inv = ROPE_BASE ** (-np.arange(half, dtype=np.float64) / half)
    base = (N_META + chunk * np.arange(seq // chunk, dtype=np.float64))[:, None, None] * inv
    off = np.arange(chunk, dtype=np.float64)[:, None] * inv
    meta = (np.arange(CHUNK, dtype=np.float64) - PAD)[:, None] * inv
    f = lambda a: jnp.asarray(a, F32)
    return (f(np.cos(base)), f(np.sin(base)), f(np.cos(off)), f(np.sin(off)),
            f(np.cos(meta)), f(np.sin(meta)))


def _cast_slab(rows, steps):
    slab = next(s for s in range(16, rows + 1, 16) if rows % s == 0 and rows // s <= steps)
    return slab, rows // slab


def _front(x2d, meta_chunk, gain, w_in, seq, to_cast):
    rows = x2d.shape[0]
    tm = FRONT_ROWS
    n_tiles = rows // tm
    tiles_per_batch = seq // tm
    cbase, sbase, coff, soff, mcos, msin = _rope_tables(seq, tm)
    decay, xi, zeta, gchunk = _retention_tables(tm)
    mzeta = _retention_tables(CHUNK)[2]
    this_tile = lambda j: (jnp.minimum(j, n_tiles - 1), 0)
    last_tile = lambda j: (jnp.maximum(j - 1, 0), 0)
    last_base = lambda j: (jnp.maximum(j - 1, 0) % tiles_per_batch, 0, 0)
    cast_in_specs, cast_out_specs = [], []
    for w in to_cast:
        slab, blocks = _cast_slab(w.shape[1], n_tiles)
        cast_in_specs.append(pl.BlockSpec(
            (1, slab, w.shape[2]), lambda j, blocks=blocks: (0, jnp.minimum(j, blocks - 1), 0)))
        cast_out_specs.append(pl.BlockSpec(
            (slab, w.shape[2]), lambda j, blocks=blocks: (jnp.minimum(j, blocks - 1), 0)))
    return pl.pallas_call(
        functools.partial(_front_kernel, tiles_per_batch, len(to_cast)),
        grid=(n_tiles + 1,),
        in_specs=[
            pl.BlockSpec((tm, D_MODEL), this_tile),
            _resident((CHUNK, D_MODEL)),
            _resident((1, D_MODEL)),
            pl.BlockSpec(memory_space=pl.ANY),
            pl.BlockSpec((1, 1, RET_DK // 2), last_base),
            pl.BlockSpec((1, 1, RET_DK // 2), last_base),
            _resident((tm, RET_DK // 2)), _resident((tm, RET_DK // 2)),
            _resident((CHUNK, RET_DK // 2)), _resident((CHUNK, RET_DK // 2)),
            _resident((RET_HEADS, CHUNK, RET_DK)),
            _resident((RET_HEADS, tm, tm)),
            _resident((RET_HEADS, tm, RET_DV)),
            _resident((RET_HEADS, tm, RET_DK)),
            _resident((RET_HEADS, 8, RET_DV)),
        ] + cast_in_specs,
        out_specs=[
            pl.BlockSpec((tm, OUT_COLS), this_tile),
            pl.BlockSpec((CHUNK, OUT_COLS), lambda j: (0, 0)),
            pl.BlockSpec((tm, RET_HEADS * RET_DV), last_tile),
        ] + cast_out_specs,
        out_shape=[
            jax.ShapeDtypeStruct((rows, OUT_COLS), BF16),
            jax.ShapeDtypeStruct((CHUNK, OUT_COLS), BF16),
            jax.ShapeDtypeStruct((rows, RET_HEADS * RET_DV), BF16),
        ] + [jax.ShapeDtypeStruct(w.shape[1:], BF16) for w in to_cast],
        scratch_shapes=[
            pltpu.VMEM((2, tm, RET_COLS), BF16),
            pltpu.VMEM((RET_HEADS, RET_DK, RET_DV), F32),
            pltpu.VMEM((CHUNK, OFF_RG - OFF_RK), BF16),
            pltpu.VMEM((D_MODEL, PROJ_WIDTH), BF16),
            pltpu.VMEM((2, D_MODEL, W_CHUNK), F32),
            pltpu.SemaphoreType.DMA((2,)),
        ],
        compiler_params=pltpu.CompilerParams(
            dimension_semantics=("arbitrary",),
            vmem_limit_bytes=VMEM_LIMIT),
        name="front",
    )(x2d, meta_chunk, gain, w_in, cbase, sbase, coff, soff, mcos, msin, mzeta, decay, xi, zeta, gchunk,
      *to_cast)


SB_GROUP = 21
SB_FILL_COLS = 256


def _softplus2(z):
    return jnp.maximum(z, jnp.log2(1.0 + jnp.exp2(jnp.minimum(z, 126.0))))


def _sb_kernel(q_ref, k_ref, v_ref, mk_ref, mv_ref, tri_ref, yr_ref, wr_ref, o_ref, ro_ref, r_ref):
    nq = q_ref.shape[1] // CHUNK
    lanes = 2 * SB_DH
    q_rows = lambda start, size: q_ref[0, pl.ds(start, size), :]
    k_rows = lambda start, size: k_ref[0, pl.ds(start, size), :]
    v_rows = lambda start, size: v_ref[0, pl.ds(start, size), :]
    meta_k = lambda: mk_ref[...]
    meta_v = lambda: mv_ref[...]
    head0_q = lax.broadcasted_iota(jnp.int32, (CHUNK, lanes), 1) < SB_DH
    t_loc = lax.broadcasted_iota(jnp.int32, (2 * CHUNK, CHUNK), 0) & (CHUNK - 1)
    s_loc = lax.broadcasted_iota(jnp.int32, (2 * CHUNK, CHUNK), 1)
    causal = s_loc < t_loc
    tri = tri_ref[...]

    def stacked_q(row0):
        qf = q_rows(row0, CHUNK).astype(F32) * (SB_DH ** -0.5 * LOG2E)
        both = jnp.concatenate([jnp.where(head0_q, qf, 0.0), jnp.where(head0_q, 0.0, qf)], axis=0)
        return both.astype(BF16)

    mask = lambda x, m: x if m is None else jnp.where(m, x, 0.0)

    def scores(qs, kw):
        return lax.dot_general(qs, kw, (((1,), (1,)), ((), ())), preferred_element_type=F32)

    def log_sticks(z, valid_lo, valid_hi):
        sp = _softplus2(z)
        ln = jnp.concatenate([mask(sp[:, :CHUNK], valid_lo), mask(sp[:, CHUNK:], valid_hi)], axis=1)
        return z - sp, ln[:, 0:1], ln.astype(BF16)

    def later_sums(ln):
        return jnp.dot(ln, tri, preferred_element_type=F32)

    def weights(log_beta, ln0, later, valid_lo, valid_hi, r):
        total = later[:, 0:1] + ln0
        if r is not None:
            later = later + r
            total = total + r
        a = jnp.exp2(log_beta - later)
        a = jnp.concatenate([mask(a[:, :CHUNK], valid_lo), mask(a[:, CHUNK:], valid_hi)], axis=1)
        return a.astype(BF16), total

    def attend(a, vw):
        both = jnp.dot(a, vw, preferred_element_type=F32)
        return jnp.where(head0_q, both[:CHUNK], both[CHUNK:])

    fill_units = (nq - 1) // SB_GROUP + 1
    fill_chunks = yr_ref.shape[1] // (fill_units * CHUNK)
    assert fill_units * fill_chunks * CHUNK == yr_ref.shape[1]

    def projection_pieces(k):
        def piece(i, c):
            rows = pl.ds(pl.multiple_of((k * fill_chunks + i) * CHUNK, CHUNK), CHUNK)
            cols = slice(c * SB_FILL_COLS, (c + 1) * SB_FILL_COLS)
            ro_ref[0, rows, cols] = jnp.dot(yr_ref[0, rows, :], wr_ref[:, cols], preferred_element_type=F32)
        return [functools.partial(piece, i, c)
                for i in range(fill_chunks) for c in range(D_MODEL // SB_FILL_COLS)]

    def windows(qs, kw, vw, valid_lo, valid_hi, r, fillers=()):
        n = len(qs)
        z, st, later, rs, outs = {}, {}, {}, [None] * n, [None] * n
        for t in range(n + 3):
            if t < n:
                z[t] = scores(qs[t], kw[t])
            if 0 <= t - 1 < n:
                st[t - 1] = log_sticks(z.pop(t - 1), valid_lo, valid_hi)
            if 0 <= t - 2 < n:
                later[t - 2] = later_sums(st[t - 2][2])
            if 0 <= t - 3 < n:
                u = t - 3
                a, rs[u] = weights(st[u][0], st[u][1], later.pop(u), valid_lo, valid_hi, r[u])
                outs[u] = attend(a, vw[u])
            if t < len(fillers):
                fillers[t]()
        for fill in fillers[n + 3:]:
            fill()
        return rs, outs

    def diagonal_windows(blocks, first, fillers=()):
        row0 = [pl.multiple_of(i * CHUNK, CHUNK) for i in blocks]
        qs = [stacked_q(r0) for r0 in row0]
        if first:
            kw = [jnp.concatenate([meta_k(), k_rows(0, CHUNK)], axis=0)]
            vw = [jnp.concatenate([meta_v(), v_rows(0, CHUNK)], axis=0)]
            valid_lo = s_loc >= PAD
        else:
            start = [pl.multiple_of(r0 - CHUNK, CHUNK) for r0 in row0]
            kw = [k_rows(s, SB_WIN) for s in start]
            vw = [v_rows(s, SB_WIN) for s in start]
            valid_lo = None
        r, out = windows(qs, kw, vw, valid_lo, causal, [None] * len(blocks), fillers)
        return qs, r, out

    def load_block(rows, meta, j):
        start = pl.multiple_of(jnp.maximum(j, 0) * CHUNK, CHUNK)
        return jnp.where(j < 0, meta(), rows(start, CHUNK))

    def earlier_windows(i, qs, r, acc):
        def window(state):
            j_hi, r, acc = state
            j_lo = j_hi - 1
            kw = jnp.concatenate([load_block(k_rows, meta_k, j_lo), load_block(k_rows, meta_k, j_hi)], axis=0)
            vw = jnp.concatenate([load_block(v_rows, meta_v, j_lo), load_block(v_rows, meta_v, j_hi)], axis=0)
            valid_lo = j_lo * CHUNK + s_loc >= -N_META
            valid_hi = j_hi * CHUNK + s_loc >= -N_META
            r, out = windows([qs], [kw], [vw], valid_lo, valid_hi, [r])
            return j_hi - 2, r[0], acc + out[0]

        def more(state):
            j_hi, r, _ = state
            return (j_hi >= -1) & (jnp.min(r) < SB_STOP)

        return lax.while_loop(more, window, window((i - 2, r, acc)))[2]

    _, _, out = diagonal_windows([0], True, projection_pieces(0))
    o_ref[0, 0:CHUNK, :] = out[0].astype(BF16)

    def store(i, acc):
        o_ref[0, pl.ds(pl.multiple_of(i * CHUNK, CHUNK), CHUNK), :] = acc.astype(BF16)

    def finish_group(g):
        rs = [r_ref[g & 1, u] for u in range(SB_GROUP)]
        unfinished = [jnp.min(r) < SB_STOP for r in rs]
        for u in range(SB_GROUP):
            @pl.when(unfinished[u])
            def _(u=u):
                i = 1 + g * SB_GROUP + u
                row0 = pl.multiple_of(i * CHUNK, CHUNK)
                acc = o_ref[0, pl.ds(row0, CHUNK), :].astype(F32)
                store(i, earlier_windows(i, stacked_q(row0), rs[u], acc))

    def group(g, r_prev):
        prev_unfinished = jnp.min(r_prev) < SB_STOP
        blocks = [1 + g * SB_GROUP + u for u in range(SB_GROUP)]
        _, r, out = diagonal_windows(blocks, False, projection_pieces(g + 1))
        for u, i in enumerate(blocks):
            store(i, out[u])
            r_ref[g & 1, u] = r[u]

        @pl.when(prev_unfinished)
        def _():
            finish_group(g - 1)

        return functools.reduce(jnp.minimum, r)

    assert (nq - 1) % SB_GROUP == 0
    n_groups = (nq - 1) // SB_GROUP
    r_last = lax.fori_loop(0, n_groups, group, jnp.full((2 * CHUNK, 1), SB_STOP, F32))

    @pl.when(jnp.min(r_last) < SB_STOP)
    def _():
        finish_group(n_groups - 1)


def _sb_tables():
    j = np.arange(SB_WIN)[:, None]
    s = np.arange(SB_WIN)[None, :]
    return jnp.asarray((j > s).astype(np.float32), BF16)


def _stickbreak(proj, mproj, yret, wr, seq):
    batch = proj.shape[0]
    lanes = 2 * SB_DH
    pairs = SB_HEADS // 2
    slab = seq // pairs
    return pl.pallas_call(
        _sb_kernel,
        grid=(batch, pairs),
        in_specs=[
            pl.BlockSpec((1, seq, lanes), lambda b, p: (b, 0, (OFF_SQ - RET_COLS) // lanes + p)),
            pl.BlockSpec((1, seq, lanes), lambda b, p: (b, 0, (OFF_SK - RET_COLS) // lanes + p)),
            pl.BlockSpec((1, seq, lanes), lambda b, p: (b, 0, (OFF_SV - RET_COLS) // lanes + p)),
            pl.BlockSpec((CHUNK, lanes), lambda b, p: (0, (OFF_SK - RET_COLS) // lanes + p)),
            pl.BlockSpec((CHUNK, lanes), lambda b, p: (0, (OFF_SV - RET_COLS) // lanes + p)),
            pl.BlockSpec((SB_WIN, SB_WIN), lambda b, p: (0, 0)),
            pl.BlockSpec((1, slab, RET_HEADS * RET_DV), lambda b, p: (b, p, 0)),
            pl.BlockSpec(wr.shape, lambda b, p: (0, 0), pipeline_mode=pl.Buffered(1)),
        ],
        out_specs=[
            pl.BlockSpec((1, seq, lanes), lambda b, p: (b, 0, p)),
            pl.BlockSpec((1, slab, D_MODEL), lambda b, p: (b, p, 0)),
        ],
        out_shape=[
            jax.ShapeDtypeStruct((batch, seq, SB_HEADS * SB_DH), BF16),
            jax.ShapeDtypeStruct((batch, seq, D_MODEL), F32),
        ],
        scratch_shapes=[pltpu.VMEM((2, SB_GROUP, 2 * CHUNK, 1), F32)],
        compiler_params=pltpu.CompilerParams(
            dimension_semantics=("parallel", "parallel"),
            vmem_limit_bytes=VMEM_LIMIT),
        name="stickbreak",
    )(proj, proj, proj, mproj, mproj, _sb_tables(), yret, wr)


FFN_COLS = 256
POST_PARTS = 2


def _post_kernel(r_ref, ys_ref, ga_ref, gb_ref, x_ref, ws_ref, wo_ref, gmix_ref,
                 gpre_ref, wi_ref, wf_ref, gpost_ref, o_ref, u_ref):
    tm = o_ref.shape[0]
    parts = [slice(p * tm // POST_PARTS, (p + 1) * tm // POST_PARTS) for p in range(POST_PARTS)]
    dot = functools.partial(jnp.dot, preferred_element_type=F32)

    r = [r_ref[p, :] for p in parts]
    s = [dot(ys_ref[p, :], ws_ref[...]) for p in parts]
    mix = []
    for i, p in enumerate(parts):
        merged = (jax.nn.sigmoid(ga_ref[p, :].astype(F32)) * r[i]
                  + jax.nn.sigmoid(gb_ref[p, :].astype(F32)) * s[i])
        mix.append(dot(merged.astype(BF16), wo_ref[...]))
    h, hn = [], []
    for i, p in enumerate(parts):
        h.append(x_ref[p, :] + _rms(mix[i], gmix_ref[...]))
        hn.append(_rms(h[i], gpre_ref[...]).astype(BF16))
    for c in range(D_FF // FFN_COLS):
        lo = c * FFN_COLS
        for i, p in enumerate(parts):
            a = dot(hn[i], wi_ref[:, lo:lo + FFN_COLS])
            b = dot(hn[i], wi_ref[:, D_FF + lo:D_FF + lo + FFN_COLS])
            u_ref[p, lo:lo + FFN_COLS] = (a * jax.nn.sigmoid(a) * b).astype(BF16)
    ff = [dot(u_ref[p, :], wf_ref[...]) for p in parts]
    for i, p in enumerate(parts):
        o_ref[p, :] = h[i] + _rms(ff[i], gpost_ref[...])


def _post(r, ysb, proj2d, x2d, ws, wo, gmix, gpre, wi, wf, gpost, tm=512):
    rows = x2d.shape[0]
    row_block = lambda width, col=0: pl.BlockSpec((tm, width), lambda i: (i, col))
    return pl.pallas_call(
        _post_kernel,
        grid=(rows // tm,),
        in_specs=[
            row_block(D_MODEL),
            row_block(D_MODEL),
            row_block(D_MODEL, (OFF_GA - RET_COLS) // D_MODEL),
            row_block(D_MODEL, (OFF_GB - RET_COLS) // D_MODEL),
            row_block(D_MODEL),
            _resident(ws.shape), _resident(wo.shape),
            _resident((1, D_MODEL)), _resident((1, D_MODEL)),
            _resident(wi.shape), _resident(wf.shape),
            _resident((1, D_MODEL)),
        ],
        out_specs=row_block(D_MODEL),
        out_shape=jax.ShapeDtypeStruct((rows, D_MODEL), F32),
        scratch_shapes=[pltpu.VMEM((tm, D_FF), BF16)],
        compiler_params=pltpu.CompilerParams(
            dimension_semantics=("parallel",),
            vmem_limit_bytes=VMEM_LIMIT),
        name="post",
    )(r, ysb, proj2d, proj2d, x2d, ws, wo, gmix, gpre, wi, wf, gpost)


def kernel(x, meta_tokens, w_in, w_ret_out, w_sb_out, w_out, w_ffn_in, w_ffn_out,
           norm_mix_pre, norm_mix_post, norm_ffn_pre, norm_ffn_post):
    batch, seq, d = x.shape
    assert d == D_MODEL and seq % 1024 == 0 and w_in.shape[0] == 1
    x2d = x.reshape(batch * seq, d)
    meta_chunk = jnp.concatenate(
        [jnp.zeros((PAD, d), x.dtype), meta_tokens.astype(x.dtype)], axis=0)

    proj2d, mproj, yret, wr, ws, wo, wi, wf = _front(
        x2d, meta_chunk, norm_mix_pre, w_in, seq,
        [w_ret_out, w_sb_out, w_out, w_ffn_in, w_ffn_out])
    ysb, r = _stickbreak(proj2d.reshape(batch, seq, OUT_COLS), mproj,
                         yret.reshape(batch, seq, -1), wr, seq)

    out = _post(r.reshape(batch * seq, -1), ysb.reshape(batch * seq, -1), proj2d, x2d, ws, wo,
                norm_mix_post, norm_ffn_pre, wi, wf, norm_ffn_post)
    return out.reshape(batch, seq, d)
```
